```python
import jax, jax.numpy as jnp
from jax import lax
import numpy as np

D_MODEL = 2048
BATCH = 4
SEQ = 4096
DEPTH = 4

N_MIXERS = 2
N_POOL_LAYERS = (DEPTH + 1) // 2
N_FOX_LAYERS = DEPTH // 2

POOL_WINDOWS = (2, 4, 8, 16)
N_POOL_GROUPS = len(POOL_WINDOWS)
POOL_GROUP = D_MODEL // N_POOL_GROUPS

HEAD_DIM = 64
N_HEADS = D_MODEL // HEAD_DIM
Q_BLOCK = 128
ATTN_SCALE = HEAD_DIM ** -0.5
FOX_IN = 4 * D_MODEL + N_HEADS

FFN_HIDDEN = ((8 * D_MODEL // 3 + 255) // 256) * 256

RMS_EPS = 1e-6

kernel_name = "hybrid_pool_fox_swiglu_trunk"


def rmsnorm(x, g):
    xf = x.astype(jnp.float32)
    y = xf * lax.rsqrt(jnp.mean(xf * xf, axis=-1, keepdims=True) + RMS_EPS)
    return (y * g.astype(jnp.float32)).astype(x.dtype)


def pool_mixer(h, w_group, scale):
    B, S, D = h.shape
    hf = h.astype(jnp.float32)
    cs = jnp.cumsum(hf, axis=1)
    n_valid = jnp.arange(1, S + 1, dtype=jnp.float32)[:, None]
    feats = []
    for g, win in enumerate(POOL_WINDOWS):
        sl = slice(g * POOL_GROUP, (g + 1) * POOL_GROUP)
        cg = cs[..., sl]
        prev = jnp.pad(cg, ((0, 0), (win, 0), (0, 0)))[:, :S]
        mean = (cg - prev) / jnp.minimum(n_valid, float(win))
        feats.append(mean - hf[..., sl])
    f = jnp.stack(feats, axis=2).astype(h.dtype)
    y = jnp.einsum('bsgc,gcd->bsgd', f, w_group).reshape(B, S, D)
    return y * scale


def fox_mixer(h, w_in, b_f, q_norm_g, k_norm_g, w_out):
    B, S, D = h.shape
    proj = h @ w_in
    q = rmsnorm(proj[..., 0 * D:1 * D].reshape(B, S, N_HEADS, HEAD_DIM), q_norm_g)
    k = rmsnorm(proj[..., 1 * D:2 * D].reshape(B, S, N_HEADS, HEAD_DIM), k_norm_g)
    v = proj[..., 2 * D:3 * D].reshape(B, S, N_HEADS, HEAD_DIM)
    og = proj[..., 3 * D:4 * D]
    log_f = jax.nn.log_sigmoid(proj[..., 4 * D:].astype(jnp.float32) + b_f.astype(jnp.float32))
    c = jnp.transpose(jnp.cumsum(log_f, axis=1), (0, 2, 1))
    outs = []
    for i in range(S // Q_BLOCK):
        qs, qe = i * Q_BLOCK, (i + 1) * Q_BLOCK
        s = jnp.einsum('bqhd,bkhd->bhqk', q[:, qs:qe], k[:, :qe],
                       preferred_element_type=jnp.float32) * ATTN_SCALE
        s = s + c[:, :, qs:qe, None] - c[:, :, None, :qe]
        causal = jnp.arange(qs, qe)[:, None] >= jnp.arange(qe)[None, :]
        s = jnp.where(causal, s, -jnp.inf)
        p = jax.nn.softmax(s, axis=-1).astype(v.dtype)
        outs.append(jnp.einsum('bhqk,bkhd->bqhd', p, v[:, :qe]))
    o = jnp.concatenate(outs, axis=1).reshape(B, S, D)
    o = o * jax.nn.sigmoid(og)
    return o @ w_out


def swiglu(h, w_gate_up, w_down):
    gu = h @ w_gate_up
    gate, up = gu[..., :FFN_HIDDEN], gu[..., FFN_HIDDEN:]
    return (jax.nn.silu(gate) * up) @ w_down


def setup_inputs(seed: int = 0) -> dict:
    key = jax.random.key(seed)
    ks = jax.random.split(key, 16)
    f32 = jnp.float32
    D = D_MODEL

    def nrm(k, shape, fan_in):
        return jax.random.normal(k, shape, f32) * (fan_in ** -0.5)

    def gain(k, shape):
        return 1.0 + 0.05 * jax.random.normal(k, shape, f32)

    return {
        "x": jax.random.normal(ks[0], (BATCH, SEQ, D), f32),
        "attn_norm_g": gain(ks[1], (DEPTH, D)),
        "ffn_norm_g": gain(ks[2], (DEPTH, D)),
        "final_norm_g": gain(ks[3], (D,)),
        "pool_w": nrm(ks[4], (N_POOL_LAYERS, N_POOL_GROUPS, POOL_GROUP, POOL_GROUP), POOL_GROUP),
        "pool_scale": gain(ks[5], (N_POOL_LAYERS, D)),
        "fox_w_in": nrm(ks[6], (N_FOX_LAYERS, D, FOX_IN), D),
        "fox_b_f": jax.random.uniform(ks[7], (N_FOX_LAYERS, N_HEADS), f32, minval=1.0, maxval=5.0),
        "fox_q_norm_g": gain(ks[8], (N_FOX_LAYERS, HEAD_DIM)),
        "fox_k_norm_g": gain(ks[9], (N_FOX_LAYERS, HEAD_DIM)),
        "fox_w_out": nrm(ks[10], (N_FOX_LAYERS, D, D), D),
        "ffn_w_gate_up": nrm(ks[11], (DEPTH, D, 2 * FFN_HIDDEN), D),
        "ffn_w_down": nrm(ks[12], (DEPTH, FFN_HIDDEN, D), FFN_HIDDEN),
    }


def reference(x, attn_norm_g, ffn_norm_g, final_norm_g, pool_w, pool_scale,
              fox_w_in, fox_b_f, fox_q_norm_g, fox_k_norm_g, fox_w_out,
              ffn_w_gate_up, ffn_w_down):
    h = x
    for i in range(DEPTH):
        hn = rmsnorm(h, attn_norm_g[i])
        j = i // N_MIXERS
        if i % N_MIXERS == 0:
            mix = pool_mixer(hn, pool_w[j], pool_scale[j])
        else:
            mix = fox_mixer(hn, fox_w_in[j], fox_b_f[j], fox_q_norm_g[j],
                            fox_k_norm_g[j], fox_w_out[j])
        h = h + mix
        h = h + swiglu(rmsnorm(h, ffn_norm_g[i]), ffn_w_gate_up[i], ffn_w_down[i])
    return rmsnorm(h, final_norm_g)
```

```python
import functools

import jax
import jax.numpy as jnp
from jax import lax
from jax.experimental import pallas as pl
from jax.experimental.pallas import tpu as pltpu

HEAD_DIM = 64
POOL_WINDOWS = (2, 4, 8, 16)
RMS_EPS = 1e-6
ATTN_SCALE = HEAD_DIM ** -0.5

POOL_HIST = 128
HEAD_ROWS = 128
AUG_ROWS = 16
HEADS_PER_STEP = 8

V7X_VMEM_LIMIT = 60000 * 1024

F32 = jnp.float32
BF16 = jnp.bfloat16


def _params(n_axes):
    return pltpu.CompilerParams(
        dimension_semantics=("arbitrary",) * n_axes,
        vmem_limit_bytes=V7X_VMEM_LIMIT,
    )


def _tile(n, want):
    t = min(n, want)
    assert n % t == 0, (n, want)
    return t


def _rms(x, g):
    ms = jnp.mean(x * x, axis=-1, keepdims=True)
    return x * lax.rsqrt(ms + RMS_EPS) * g


def _split3(x):
    x1 = x.astype(BF16).astype(F32)
    r = x - x1
    x2 = r.astype(BF16).astype(F32)
    x3 = (r - x2).astype(BF16).astype(F32)
    return x1, x2, x3


def _ffn_kernel(x_ref, g_ref, wg_ref, wu_ref, wd_ref, *rest, final_norm):
    if final_norm:
        fg_ref, o_ref, xn_ref = rest
    else:
        o_ref, xn_ref = rest
    j = pl.program_id(1)

    @pl.when(j == 0)
    def _():
        x = x_ref[...]
        xn_ref[...] = _rms(x, g_ref[...]).astype(BF16)
        o_ref[...] = x

    xn = xn_ref[...]
    gate = jnp.dot(xn, wg_ref[...], preferred_element_type=F32)
    up = jnp.dot(xn, wu_ref[...], preferred_element_type=F32)
    a = (gate * jax.nn.sigmoid(gate) * up).astype(BF16)
    o_ref[...] += jnp.dot(a, wd_ref[...], preferred_element_type=F32)

    if final_norm:
        @pl.when(j == pl.num_programs(1) - 1)
        def _():
            o_ref[...] = _rms(o_ref[...], fg_ref[...])


def _ffn(h, g, w_gu, w_d, final_g=None):
    M, D = h.shape
    F = w_d.shape[0]
    tm = _tile(M, 512)
    tf = _tile(F, 512)
    nf = F // tf
    final_norm = final_g is not None
    in_specs = [
        pl.BlockSpec((tm, D), lambda i, j: (i, 0)),
        pl.BlockSpec((1, D), lambda i, j: (0, 0)),
        pl.BlockSpec((D, tf), lambda i, j: (0, j)),
        pl.BlockSpec((D, tf), lambda i, j: (0, j + nf)),
        pl.BlockSpec((tf, D), lambda i, j: (j, 0)),
    ]
    args = [h, g.reshape(1, D), w_gu, w_gu, w_d]
    if final_norm:
        in_specs.append(pl.BlockSpec((1, D), lambda i, j: (0, 0)))
        args.append(final_g.reshape(1, D))
    return pl.pallas_call(
        functools.partial(_ffn_kernel, final_norm=final_norm),
        grid=(M // tm, nf),
        in_specs=in_specs,
        out_specs=pl.BlockSpec((tm, D), lambda i, j: (i, 0)),
        out_shape=jax.ShapeDtypeStruct((M, D), F32),
        scratch_shapes=[pltpu.VMEM((tm, D), BF16)],
        compiler_params=_params(2),
        name="ffn",
    )(*args)


def _pool_kernel(x_ref, g_ref, band_ref, cnt_ref, w_ref, sc_ref, o_ref, hi_ref, lo_ref, *,
                 n_groups):
    si = pl.program_id(1)
    tm, D = x_ref.shape
    C = D // n_groups

    @pl.when(si == 0)
    def _():
        hi_ref[:POOL_HIST, :] = jnp.zeros((POOL_HIST, D), BF16)
        lo_ref[:POOL_HIST, :] = jnp.zeros((POOL_HIST, D), BF16)

    x = x_ref[...]
    hn = _rms(x, g_ref[...])
    hi = hn.astype(BF16)
    hi_ref[POOL_HIST:, :] = hi
    lo_ref[POOL_HIST:, :] = (hn - hi.astype(F32)).astype(BF16)

    for gi in range(n_groups):
        cols = slice(gi * C, (gi + 1) * C)
        band = band_ref[gi]
        win = (jnp.dot(band, hi_ref[:, cols], preferred_element_type=F32)
               + jnp.dot(band, lo_ref[:, cols], preferred_element_type=F32))
        feat = win / cnt_ref[:, gi:gi + 1] - hn[:, cols]
        y = jnp.dot(feat.astype(BF16), w_ref[gi], preferred_element_type=F32)
        o_ref[:, cols] = x[:, cols] + y * sc_ref[:, cols]

    hi_ref[:POOL_HIST, :] = hi_ref[tm:, :]
    lo_ref[:POOL_HIST, :] = lo_ref[tm:, :]


def _pool_layer(h, g, w_group, scale):
    B, S, D = h.shape
    G = len(POOL_WINDOWS)
    C = D // G
    tm = _tile(S, 256)
    assert tm >= POOL_HIST >= max(POOL_WINDOWS)
    r = jnp.arange(tm)[:, None] + POOL_HIST
    c = jnp.arange(POOL_HIST + tm)[None, :]
    band = jnp.stack([((c <= r) & (c > r - w)) for w in POOL_WINDOWS]).astype(BF16)
    n_valid = jnp.arange(1, S + 1, dtype=F32)[:, None]
    cnt = jnp.minimum(n_valid, jnp.asarray(POOL_WINDOWS, F32)[None, :])
    return pl.pallas_call(
        functools.partial(_pool_kernel, n_groups=G),
        grid=(B, S // tm),
        in_specs=[
            pl.BlockSpec((None, tm, D), lambda b, s: (b, s, 0)),
            pl.BlockSpec((1, D), lambda b, s: (0, 0)),
            pl.BlockSpec((G, tm, POOL_HIST + tm), lambda b, s: (0, 0, 0)),
            pl.BlockSpec((tm, G), lambda b, s: (s, 0)),
            pl.BlockSpec((G, C, C), lambda b, s: (0, 0, 0)),
            pl.BlockSpec((1, D), lambda b, s: (0, 0)),
        ],
        out_specs=pl.BlockSpec((None, tm, D), lambda b, s: (b, s, 0)),
        out_shape=jax.ShapeDtypeStruct((B, S, D), F32),
        scratch_shapes=[
            pltpu.VMEM((POOL_HIST + tm, D), BF16), pltpu.VMEM((POOL_HIST + tm, D), BF16),
        ],
        compiler_params=_params(2),
        name="pool",
    )(h, g.reshape(1, D), band, cnt, w_group, scale.reshape(1, D))


def _qk_proj_kernel(x_ref, g_ref, w_ref, wf_ref, bf_ref, tri_ref, gq_ref, gk_ref, o_ref,
                    xn_ref, c_ref, carry_ref, *, n_q_steps):
    si = pl.program_id(1)
    j = pl.program_id(2)
    tm = x_ref.shape[0]
    n_heads = wf_ref.shape[0]

    @pl.when(j == 0)
    def _():
        xn = _rms(x_ref[...], g_ref[...]).astype(BF16)
        xn_ref[...] = xn
        z = lax.dot_general(wf_ref[...], xn, (((1,), (1,)), ((), ())),
                            preferred_element_type=F32) + bf_ref[...]
        logf = jnp.minimum(z, 0.0) - jnp.log1p(jnp.exp(-jnp.abs(z)))
        f1, f2, f3 = _split3(logf)
        tri = tri_ref[...]
        csum = (jnp.dot(f1.astype(BF16), tri, preferred_element_type=F32)
                + jnp.dot(f2.astype(BF16), tri, preferred_element_type=F32)
                + jnp.dot(f3.astype(BF16), tri, preferred_element_type=F32))
        prev = jnp.where(si == 0, 0.0, carry_ref[...])
        csum = csum + prev
        carry_ref[...] = csum[:, tm - 1:tm]
        c1, c2, c3 = _split3(csum)
        c_ref[0] = c1
        c_ref[1] = c2
        c_ref[2] = c3

    y = lax.dot_general(w_ref[...], xn_ref[...], (((1,), (1,)), ((), ())),
                        preferred_element_type=F32)
    is_q = j < n_q_steps
    gain = jnp.where(is_q, gq_ref[...] * ATTN_SCALE, gk_ref[...])
    sign = jnp.where(is_q, 1.0, -1.0)
    c_row0 = jnp.where(is_q, 0, 3)
    head0 = jnp.where(is_q, j, j - n_q_steps) * HEADS_PER_STEP
    row = lax.broadcasted_iota(jnp.int32, (AUG_ROWS, tm), 0)
    for hh in range(HEADS_PER_STEP):
        yh = y[hh * HEAD_DIM:(hh + 1) * HEAD_DIM]
        ms = jnp.mean(yh * yh, axis=0, keepdims=True)
        yn = yh * lax.rsqrt(ms + RMS_EPS) * gain
        hd = head0 + hh
        c1 = c_ref[0, pl.ds(hd, 1), :] * sign
        c2 = c_ref[1, pl.ds(hd, 1), :] * sign
        c3 = c_ref[2, pl.ds(hd, 1), :] * sign
        sel = row - c_row0
        in_c = (sel >= 0) & (sel < 3)
        in_one = (row >= 3 - c_row0) & (row < 6 - c_row0)
        cval = jnp.where(sel == 0, c1, jnp.where(sel == 1, c2, c3))
        aug = jnp.where(in_c, cval, jnp.where(in_one, 1.0, 0.0))
        base = hh * HEAD_ROWS
        o_ref[base:base + HEAD_DIM, :] = yn.astype(BF16)
        o_ref[base + HEAD_DIM:base + HEAD_DIM + AUG_ROWS, :] = aug.astype(BF16)
        o_ref[base + HEAD_DIM + AUG_ROWS:base + HEAD_ROWS, :] = jnp.zeros(
            (HEAD_ROWS - HEAD_DIM - AUG_ROWS, tm), BF16)


def _qk_proj(h, g, w_qk_t, w_f_t, b_f, gq, gk):
    B, S, D = h.shape
    H = D // HEAD_DIM
    tm = _tile(S, 512)
    rows = HEADS_PER_STEP * HEAD_DIM
    assert D % rows == 0
    n_q_steps = D // rows
    tri = (jnp.arange(tm)[:, None] <= jnp.arange(tm)[None, :]).astype(BF16)
    return pl.pallas_call(
        functools.partial(_qk_proj_kernel, n_q_steps=n_q_steps),
        grid=(B, S // tm, 2 * n_q_steps),
        in_specs=[
            pl.BlockSpec((None, tm, D), lambda b, s, j: (b, s, 0)),
            pl.BlockSpec((1, D), lambda b, s, j: (0, 0)),
            pl.BlockSpec((rows, D), lambda b, s, j: (j, 0)),
            pl.BlockSpec((H, D), lambda b, s, j: (0, 0)),
            pl.BlockSpec((H, 1), lambda b, s, j: (0, 0)),
            pl.BlockSpec((tm, tm), lambda b, s, j: (0, 0)),
            pl.BlockSpec((HEAD_DIM, 1), lambda b, s, j: (0, 0)),
            pl.BlockSpec((HEAD_DIM, 1), lambda b, s, j: (0, 0)),
        ],
        out_specs=pl.BlockSpec((None, HEADS_PER_STEP * HEAD_ROWS, tm), lambda b, s, j: (b, j, s)),
        out_shape=jax.ShapeDtypeStruct((B, 2 * H * HEAD_ROWS, S), BF16),
        scratch_shapes=[
            pltpu.VMEM((tm, D), BF16),
            pltpu.VMEM((3, H, tm), F32),
            pltpu.VMEM((H, 1), F32),
        ],
        compiler_params=_params(3),
        name="qk_proj",
    )(h, g.reshape(1, D), w_qk_t, w_f_t, b_f.reshape(H, 1), tri,
      gq.reshape(HEAD_DIM, 1), gk.reshape(HEAD_DIM, 1))


def _vg_proj_kernel(x_ref, g_ref, w_ref, o_ref, xn_ref, *, n_v_steps):
    j = pl.program_id(2)

    @pl.when(j == 0)
    def _():
        xn_ref[...] = _rms(x_ref[...], g_ref[...]).astype(BF16)

    y = lax.dot_general(w_ref[...], xn_ref[...], (((1,), (1,)), ((), ())),
                        preferred_element_type=F32)

    @pl.when(j < n_v_steps)
    def _():
        o_ref[...] = y.astype(BF16)

    @pl.when(j >= n_v_steps)
    def _():
        o_ref[...] = jax.nn.sigmoid(y).astype(BF16)


def _vg_proj(h, g, w_vg_t):
    B, S, D = h.shape
    tm = _tile(S, 512)
    tn = _tile(D, 512)
    n_v_steps = D // tn
    return pl.pallas_call(
        functools.partial(_vg_proj_kernel, n_v_steps=n_v_steps),
        grid=(B, S // tm, 2 * n_v_steps),
        in_specs=[
            pl.BlockSpec((None, tm, D), lambda b, s, j: (b, s, 0)),
            pl.BlockSpec((1, D), lambda b, s, j: (0, 0)),
            pl.BlockSpec((tn, D), lambda b, s, j: (j, 0)),
        ],
        out_specs=pl.BlockSpec((None, tn, tm), lambda b, s, j: (b, j, s)),
        out_shape=jax.ShapeDtypeStruct((B, 2 * D, S), BF16),
        scratch_shapes=[pltpu.VMEM((tm, D), BF16)],
        compiler_params=_params(3),
        name="vg_proj",
    )(h, g.reshape(1, D), w_vg_t)


def _attn_kernel(q_ref, k_ref, v_ref, gate_ref, o_ref, kn_ref, vb_ref, *, blk):
    S = q_ref.shape[1]
    n_blk = S // blk

    for c in range(n_blk):
        cs = slice(c * blk, (c + 1) * blk)
        kn_ref[c] = k_ref[:, cs].astype(F32).T.astype(BF16)
        vb_ref[c] = v_ref[:, cs]

    kk = lax.broadcasted_iota(jnp.int32, (blk, blk), 0)
    qq = lax.broadcasted_iota(jnp.int32, (blk, blk), 1)
    causal = kk <= qq

    def step(q_t, k_blk, v_t, carry, mask):
        m, l, acc = carry
        s_t = jnp.dot(k_blk, q_t, preferred_element_type=F32)
        if mask:
            s_t = jnp.where(causal, s_t, -jnp.inf)
        m_new = jnp.maximum(m, jnp.max(s_t, axis=0, keepdims=True))
        alpha = jnp.exp(m - m_new)
        p = jnp.exp(s_t - m_new)
        l = alpha * l + jnp.sum(p, axis=0, keepdims=True)
        acc = alpha * acc + jnp.dot(v_t, p.astype(BF16), preferred_element_type=F32)
        return m_new, l, acc

    for qi in range(n_blk):
        qs = slice(qi * blk, (qi + 1) * blk)
        q_t = q_ref[:, qs]
        carry = (jnp.full((1, blk), -jnp.inf, F32), jnp.zeros((1, blk), F32),
                 jnp.zeros((HEAD_DIM, blk), F32))

        def body(kj, carry, q_t=q_t):
            return step(q_t, kn_ref[kj], vb_ref[kj], carry, mask=False)

        carry = lax.fori_loop(0, qi, body, carry)
        m, l, acc = step(q_t, kn_ref[qi], vb_ref[qi], carry, mask=True)
        o_ref[:, qs] = (acc / l * gate_ref[:, qs].astype(F32)).astype(BF16)


def _attention(qk_t, vg_t, n_heads):
    B, _, S = qk_t.shape
    D = n_heads * HEAD_DIM
    blk = _tile(S, 512)
    return pl.pallas_call(
        functools.partial(_attn_kernel, blk=blk),
        grid=(B, n_heads),
        in_specs=[
            pl.BlockSpec((None, HEAD_ROWS, S), lambda b, h: (b, h, 0)),
            pl.BlockSpec((None, HEAD_ROWS, S), lambda b, h: (b, h + n_heads, 0)),
            pl.BlockSpec((None, HEAD_DIM, S), lambda b, h: (b, h, 0)),
            pl.BlockSpec((None, HEAD_DIM, S), lambda b, h: (b, h + n_heads, 0)),
        ],
        out_specs=pl.BlockSpec((None, HEAD_DIM, S), lambda b, h: (b, h, 0)),
        out_shape=jax.ShapeDtypeStruct((B, D, S), BF16),
        scratch_shapes=[pltpu.VMEM((S // blk, blk, HEAD_ROWS), BF16),
                        pltpu.VMEM((S // blk, HEAD_DIM, blk), BF16)],
        compiler_params=_params(2),
        name="fox_attn",
    )(qk_t, qk_t, vg_t, vg_t)


def _out_proj_kernel(o_ref, w_ref, x_ref, y_ref):
    y_t = jnp.dot(w_ref[...], o_ref[...], preferred_element_type=F32)
    y_ref[...] = x_ref[...] + y_t.T


def _out_proj(o_t, w_out_t, h):
    B, S, D = h.shape
    tm = _tile(S, 512)
    return pl.pallas_call(
        _out_proj_kernel,
        grid=(B, S // tm),
        in_specs=[
            pl.BlockSpec((None, D, tm), lambda b, s: (b, 0, s)),
            pl.BlockSpec((D, D), lambda b, s: (0, 0)),
            pl.BlockSpec((None, tm, D), lambda b, s: (b, s, 0)),
        ],
        out_specs=pl.BlockSpec((None, tm, D), lambda b, s: (b, s, 0)),
        out_shape=jax.ShapeDtypeStruct((B, S, D), F32),
        compiler_params=_params(2),
        name="out_proj",
    )(o_t, w_out_t, h)


def _fox_layer(h, g, w_in, b_f, gq, gk, w_out):
    B, S, D = h.shape
    H = D // HEAD_DIM
    w_qk_t = w_in[:, :2 * D].T.astype(BF16)
    w_vg_t = w_in[:, 2 * D:4 * D].T.astype(BF16)
    w_f_t = w_in[:, 4 * D:].T.astype(BF16)
    qk_t = _qk_proj(h, g, w_qk_t, w_f_t, b_f, gq, gk)
    vg_t = _vg_proj(h, g, w_vg_t)
    o_t = _attention(qk_t, vg_t, H)
    return _out_proj(o_t, w_out.T.astype(BF16), h)


def kernel(x, attn_norm_g, ffn_norm_g, final_norm_g, pool_w, pool_scale, fox_w_in, fox_b_f,
           fox_q_norm_g, fox_k_norm_g, fox_w_out, ffn_w_gate_up, ffn_w_down):
    B, S, D = x.shape
    depth = attn_norm_g.shape[0]
    h = x
    for i in range(depth):
        j = i // 2
        if i % 2 == 0:
            h = _pool_layer(h, attn_norm_g[i], pool_w[j].astype(BF16), pool_scale[j])
        else:
            h = _fox_layer(h, attn_norm_g[i], fox_w_in[j], fox_b_f[j], fox_q_norm_g[j],
                           fox_k_norm_g[j], fox_w_out[j])
        final_g = final_norm_g if i == depth - 1 else None
        h = _ffn(h.reshape(B * S, D), ffn_norm_g[i], ffn_w_gate_up[i].astype(BF16),
                 ffn_w_down[i].astype(BF16), final_g).reshape(B, S, D)
    return h
```

```python
import functools

import jax
import jax.numpy as jnp
from jax import lax
from jax.experimental import pallas as pl
from jax.experimental.pallas import tpu as pltpu

HEAD_DIM = 64
POOL_WINDOWS = (2, 4, 8, 16)
RMS_EPS = 1e-6
ATTN_SCALE = HEAD_DIM ** -0.5

POOL_HIST = 128
HEAD_ROWS = 128
AUG_ROWS = 16
HEADS_PER_STEP = 8
MAX_SHIFT = 30.0
BOUND_SLACK = 1.0 + 2.0 ** -6
MASKED_LOGIT = -1e30

V7X_VMEM_LIMIT = 60000 * 1024

F32 = jnp.float32
BF16 = jnp.bfloat16


def _params(n_axes):
    return pltpu.CompilerParams(
        dimension_semantics=("arbitrary",) * n_axes,
        vmem_limit_bytes=V7X_VMEM_LIMIT,
    )


def _tile(n, want):
    t = min(n, want)
    assert n % t == 0, (n, want)
    return t


def _rms(x, g):
    ms = jnp.mean(x * x, axis=-1, keepdims=True)
    return x * lax.rsqrt(ms + RMS_EPS) * g


def _split3(x):
    x1 = x.astype(BF16).astype(F32)
    r = x - x1
    x2 = r.astype(BF16).astype(F32)
    x3 = (r - x2).astype(BF16).astype(F32)
    return x1, x2, x3


def _ffn_kernel(x_ref, g_ref, wg_ref, wu_ref, wd_ref, *rest, final_norm):
    if final_norm:
        fg_ref, o_ref, xn_ref = rest
    else:
        o_ref, xn_ref = rest
    j = pl.program_id(1)

    @pl.when(j == 0)
    def _():
        x = x_ref[...]
        xn_ref[...] = _rms(x, g_ref[...]).astype(BF16)
        o_ref[...] = x

    xn = xn_ref[...]
    gate = jnp.dot(xn, wg_ref[...], preferred_element_type=F32)
    up = jnp.dot(xn, wu_ref[...], preferred_element_type=F32)
    a = (gate * jax.nn.sigmoid(gate) * up).astype(BF16)
    o_ref[...] += jnp.dot(a, wd_ref[...], preferred_element_type=F32)

    if final_norm:
        @pl.when(j == pl.num_programs(1) - 1)
        def _():
            o_ref[...] = _rms(o_ref[...], fg_ref[...])


def _ffn(h, g, w_gu, w_d, final_g=None):
    M, D = h.shape
    F = w_d.shape[0]
    tm = _tile(M, 512)
    tf = _tile(F, 512)
    nf = F // tf
    final_norm = final_g is not None
    in_specs = [
        pl.BlockSpec((tm, D), lambda i, j: (i, 0)),
        pl.BlockSpec((1, D), lambda i, j: (0, 0)),
        pl.BlockSpec((D, tf), lambda i, j: (0, j)),
        pl.BlockSpec((D, tf), lambda i, j: (0, j + nf)),
        pl.BlockSpec((tf, D), lambda i, j: (j, 0)),
    ]
    args = [h, g.reshape(1, D), w_gu, w_gu, w_d]
    if final_norm:
        in_specs.append(pl.BlockSpec((1, D), lambda i, j: (0, 0)))
        args.append(final_g.reshape(1, D))
    return pl.pallas_call(
        functools.partial(_ffn_kernel, final_norm=final_norm),
        grid=(M // tm, nf),
        in_specs=in_specs,
        out_specs=pl.BlockSpec((tm, D), lambda i, j: (i, 0)),
        out_shape=jax.ShapeDtypeStruct((M, D), F32),
        scratch_shapes=[pltpu.VMEM((tm, D), BF16)],
        compiler_params=_params(2),
        name="ffn",
    )(*args)


def _pool_kernel(x_ref, g_ref, band_ref, cnt_ref, w_ref, sc_ref, o_ref, hi_ref, lo_ref, *,
                 n_groups):
    si = pl.program_id(1)
    tm, D = x_ref.shape
    C = D // n_groups

    @pl.when(si == 0)
    def _():
        hi_ref[:POOL_HIST, :] = jnp.zeros((POOL_HIST, D), BF16)
        lo_ref[:POOL_HIST, :] = jnp.zeros((POOL_HIST, D), BF16)

    x = x_ref[...]
    hn = _rms(x, g_ref[...])
    hi = hn.astype(BF16)
    hi_ref[POOL_HIST:, :] = hi
    lo_ref[POOL_HIST:, :] = (hn - hi.astype(F32)).astype(BF16)

    for gi in range(n_groups):
        cols = slice(gi * C, (gi + 1) * C)
        band = band_ref[gi]
        win = (jnp.dot(band, hi_ref[:, cols], preferred_element_type=F32)
               + jnp.dot(band, lo_ref[:, cols], preferred_element_type=F32))
        feat = win / cnt_ref[:, gi:gi + 1] - hn[:, cols]
        y = jnp.dot(feat.astype(BF16), w_ref[gi], preferred_element_type=F32)
        o_ref[:, cols] = x[:, cols] + y * sc_ref[:, cols]

    hi_ref[:POOL_HIST, :] = hi_ref[tm:, :]
    lo_ref[:POOL_HIST, :] = lo_ref[tm:, :]


def _pool_layer(h, g, w_group, scale):
    B, S, D = h.shape
    G = len(POOL_WINDOWS)
    C = D // G
    tm = _tile(S, 256)
    assert tm >= POOL_HIST >= max(POOL_WINDOWS)
    r = jnp.arange(tm)[:, None] + POOL_HIST
    c = jnp.arange(POOL_HIST + tm)[None, :]
    band = jnp.stack([((c <= r) & (c > r - w)) for w in POOL_WINDOWS]).astype(BF16)
    n_valid = jnp.arange(1, S + 1, dtype=F32)[:, None]
    cnt = jnp.minimum(n_valid, jnp.asarray(POOL_WINDOWS, F32)[None, :])
    return pl.pallas_call(
        functools.partial(_pool_kernel, n_groups=G),
        grid=(B, S // tm),
        in_specs=[
            pl.BlockSpec((None, tm, D), lambda b, s: (b, s, 0)),
            pl.BlockSpec((1, D), lambda b, s: (0, 0)),
            pl.BlockSpec((G, tm, POOL_HIST + tm), lambda b, s: (0, 0, 0)),
            pl.BlockSpec((tm, G), lambda b, s: (s, 0)),
            pl.BlockSpec((G, C, C), lambda b, s: (0, 0, 0)),
            pl.BlockSpec((1, D), lambda b, s: (0, 0)),
        ],
        out_specs=pl.BlockSpec((None, tm, D), lambda b, s: (b, s, 0)),
        out_shape=jax.ShapeDtypeStruct((B, S, D), F32),
        scratch_shapes=[
            pltpu.VMEM((POOL_HIST + tm, D), BF16), pltpu.VMEM((POOL_HIST + tm, D), BF16),
        ],
        compiler_params=_params(2),
        name="pool",
    )(h, g.reshape(1, D), band, cnt, w_group, scale.reshape(1, D))


def _qk_proj_kernel(x_ref, g_ref, w_ref, wf_ref, bf_ref, tri_ref, gq_ref, gk_ref, shift_ref,
                    o_ref, xn_ref, c_ref, carry_ref, *, n_q_steps):
    si = pl.program_id(1)
    j = pl.program_id(2)
    tm = x_ref.shape[0]

    @pl.when(j == 0)
    def _():
        xn = _rms(x_ref[...], g_ref[...]).astype(BF16)
        xn_ref[...] = xn
        z = lax.dot_general(wf_ref[...], xn, (((1,), (1,)), ((), ())),
                            preferred_element_type=F32) + bf_ref[...]
        logf = jnp.minimum(z, 0.0) - jnp.log1p(jnp.exp(-jnp.abs(z)))
        f1, f2, f3 = _split3(logf)
        tri = tri_ref[...]
        csum = (jnp.dot(f1.astype(BF16), tri, preferred_element_type=F32)
                + jnp.dot(f2.astype(BF16), tri, preferred_element_type=F32)
                + jnp.dot(f3.astype(BF16), tri, preferred_element_type=F32))
        prev = jnp.where(si == 0, 0.0, carry_ref[...])
        csum = csum + prev
        carry_ref[...] = csum[:, tm - 1:tm]
        for side, val in ((0, csum - shift_ref[...]), (1, -csum)):
            for part, term in enumerate(_split3(val)):
                c_ref[side, part] = term

    y = lax.dot_general(w_ref[...], xn_ref[...], (((1,), (1,)), ((), ())),
                        preferred_element_type=F32)
    is_q = j < n_q_steps
    gain = jnp.where(is_q, gq_ref[...] * ATTN_SCALE, gk_ref[...])
    side = jnp.where(is_q, 0, 1)
    c_row0 = jnp.where(is_q, 0, 3)
    head0 = jnp.where(is_q, j, j - n_q_steps) * HEADS_PER_STEP
    row = lax.broadcasted_iota(jnp.int32, (AUG_ROWS, tm), 0)
    for hh in range(HEADS_PER_STEP):
        yh = y[hh * HEAD_DIM:(hh + 1) * HEAD_DIM]
        ms = jnp.mean(yh * yh, axis=0, keepdims=True)
        yn = yh * lax.rsqrt(ms + RMS_EPS) * gain
        hd = head0 + hh
        c1 = c_ref[side, 0, pl.ds(hd, 1), :]
        c2 = c_ref[side, 1, pl.ds(hd, 1), :]
        c3 = c_ref[side, 2, pl.ds(hd, 1), :]
        sel = row - c_row0
        in_c = (sel >= 0) & (sel < 3)
        in_one = (row >= 3 - c_row0) & (row < 6 - c_row0)
        cval = jnp.where(sel == 0, c1, jnp.where(sel == 1, c2, c3))
        aug = jnp.where(in_c, cval, jnp.where(in_one, 1.0, 0.0))
        base = hh * HEAD_ROWS
        o_ref[base:base + HEAD_DIM, :] = yn.astype(BF16)
        o_ref[base + HEAD_DIM:base + HEAD_DIM + AUG_ROWS, :] = aug.astype(BF16)
        o_ref[base + HEAD_DIM + AUG_ROWS:base + HEAD_ROWS, :] = jnp.zeros(
            (HEAD_ROWS - HEAD_DIM - AUG_ROWS, tm), BF16)


def _qk_proj(h, g, w_qk_t, w_f_t, b_f, gq, gk, shift):
    B, S, D = h.shape
    H = D // HEAD_DIM
    tm = _tile(S, 512)
    rows = HEADS_PER_STEP * HEAD_DIM
    assert D % rows == 0
    n_q_steps = D // rows
    tri = (jnp.arange(tm)[:, None] <= jnp.arange(tm)[None, :]).astype(BF16)
    return pl.pallas_call(
        functools.partial(_qk_proj_kernel, n_q_steps=n_q_steps),
        grid=(B, S // tm, 2 * n_q_steps),
        in_specs=[
            pl.BlockSpec((None, tm, D), lambda b, s, j: (b, s, 0)),
            pl.BlockSpec((1, D), lambda b, s, j: (0, 0)),
            pl.BlockSpec((rows, D), lambda b, s, j: (j, 0)),
            pl.BlockSpec((H, D), lambda b, s, j: (0, 0)),
            pl.BlockSpec((H, 1), lambda b, s, j: (0, 0)),
            pl.BlockSpec((tm, tm), lambda b, s, j: (0, 0)),
            pl.BlockSpec((HEAD_DIM, 1), lambda b, s, j: (0, 0)),
            pl.BlockSpec((HEAD_DIM, 1), lambda b, s, j: (0, 0)),
            pl.BlockSpec((1, 1), lambda b, s, j: (0, 0)),
        ],
        out_specs=pl.BlockSpec((None, HEADS_PER_STEP * HEAD_ROWS, tm), lambda b, s, j: (b, j, s)),
        out_shape=jax.ShapeDtypeStruct((B, 2 * H * HEAD_ROWS, S), BF16),
        scratch_shapes=[
            pltpu.VMEM((tm, D), BF16),
            pltpu.VMEM((2, 3, H, tm), F32),
            pltpu.VMEM((H, 1), F32),
        ],
        compiler_params=_params(3),
        name="qk_proj",
    )(h, g.reshape(1, D), w_qk_t, w_f_t, b_f.reshape(H, 1), tri,
      gq.reshape(HEAD_DIM, 1), gk.reshape(HEAD_DIM, 1), shift.reshape(1, 1))


def _vg_proj_kernel(x_ref, g_ref, w_ref, o_ref, xn_ref, *, n_v_steps):
    j = pl.program_id(2)

    @pl.when(j == 0)
    def _():
        xn_ref[...] = _rms(x_ref[...], g_ref[...]).astype(BF16)

    y = lax.dot_general(w_ref[...], xn_ref[...], (((1,), (1,)), ((), ())),
                        preferred_element_type=F32)

    @pl.when(j < n_v_steps)
    def _():
        o_ref[...] = y.astype(BF16)

    @pl.when(j >= n_v_steps)
    def _():
        o_ref[...] = jax.nn.sigmoid(y).astype(BF16)


def _vg_proj(h, g, w_vg_t):
    B, S, D = h.shape
    tm = _tile(S, 512)
    tn = _tile(D, 512)
    n_v_steps = D // tn
    return pl.pallas_call(
        functools.partial(_vg_proj_kernel, n_v_steps=n_v_steps),
        grid=(B, S // tm, 2 * n_v_steps),
        in_specs=[
            pl.BlockSpec((None, tm, D), lambda b, s, j: (b, s, 0)),
            pl.BlockSpec((1, D), lambda b, s, j: (0, 0)),
            pl.BlockSpec((tn, D), lambda b, s, j: (j, 0)),
        ],
        out_specs=pl.BlockSpec((None, tn, tm), lambda b, s, j: (b, j, s)),
        out_shape=jax.ShapeDtypeStruct((B, 2 * D, S), BF16),
        scratch_shapes=[pltpu.VMEM((tm, D), BF16)],
        compiler_params=_params(3),
        name="vg_proj",
    )(h, g.reshape(1, D), w_vg_t)


def _attn_kernel(q_ref, k_ref, v_ref, gate_ref, o_ref, kn_ref, vb_ref, *, blk, online_max):
    S = q_ref.shape[1]
    n_blk = S // blk

    for c in range(n_blk):
        cs = slice(c * blk, (c + 1) * blk)
        kn_ref[c] = k_ref[:, cs].astype(F32).T.astype(BF16)
        vb_ref[c] = v_ref[:, cs]

    kk = lax.broadcasted_iota(jnp.int32, (blk, blk), 0)
    qq = lax.broadcasted_iota(jnp.int32, (blk, blk), 1)
    causal = kk <= qq

    def step_online(q_t, k_blk, v_t, carry, mask):
        m, l, acc = carry
        s_t = jnp.dot(k_blk, q_t, preferred_element_type=F32)
        if mask:
            s_t = jnp.where(causal, s_t, -jnp.inf)
        m_new = jnp.maximum(m, jnp.max(s_t, axis=0, keepdims=True))
        alpha = jnp.exp(m - m_new)
        p = jnp.exp(s_t - m_new)
        l = alpha * l + jnp.sum(p, axis=0, keepdims=True)
        acc = alpha * acc + jnp.dot(v_t, p.astype(BF16), preferred_element_type=F32)
        return m_new, l, acc

    def step_shifted(q_t, k_blk, v_t, carry, mask):
        l8, acc = carry
        s_t = jnp.dot(k_blk, q_t, preferred_element_type=F32)
        if mask:
            s_t = jnp.where(causal, s_t, MASKED_LOGIT)
        p = jnp.exp(s_t)
        l8 = l8 + jnp.sum(p.reshape(blk // 8, 8, blk), axis=0)
        acc = acc + jnp.dot(v_t, p.astype(BF16), preferred_element_type=F32)
        return l8, acc

    step = step_online if online_max else step_shifted
    for qi in range(n_blk):
        qs = slice(qi * blk, (qi + 1) * blk)
        q_t = q_ref[:, qs]
        acc0 = jnp.zeros((HEAD_DIM, blk), F32)
        if online_max:
            carry = (jnp.full((1, blk), -jnp.inf, F32), jnp.zeros((1, blk), F32), acc0)
        else:
            carry = (jnp.zeros((8, blk), F32), acc0)

        def body(kj, carry, q_t=q_t):
            return step(q_t, kn_ref[kj], vb_ref[kj], carry, mask=False)

        carry = lax.fori_loop(0, qi, body, carry, unroll=not online_max)
        carry = step(q_t, kn_ref[qi], vb_ref[qi], carry, mask=True)
        acc = carry[-1]
        l = carry[-2] if online_max else jnp.sum(carry[0], axis=0, keepdims=True)
        o_ref[:, qs] = (acc / l * gate_ref[:, qs].astype(F32)).astype(BF16)


def _attention(qk_t, vg_t, n_heads, online_max):
    B, _, S = qk_t.shape
    D = n_heads * HEAD_DIM
    blk = _tile(S, 512)
    return pl.pallas_call(
        functools.partial(_attn_kernel, blk=blk, online_max=online_max),
        grid=(B, n_heads),
        in_specs=[
            pl.BlockSpec((None, HEAD_ROWS, S), lambda b, h: (b, h, 0)),
            pl.BlockSpec((None, HEAD_ROWS, S), lambda b, h: (b, h + n_heads, 0)),
            pl.BlockSpec((None, HEAD_DIM, S), lambda b, h: (b, h, 0)),
            pl.BlockSpec((None, HEAD_DIM, S), lambda b, h: (b, h + n_heads, 0)),
        ],
        out_specs=pl.BlockSpec((None, HEAD_DIM, S), lambda b, h: (b, h, 0)),
        out_shape=jax.ShapeDtypeStruct((B, D, S), BF16),
        scratch_shapes=[pltpu.VMEM((S // blk, blk, HEAD_ROWS), BF16),
                        pltpu.VMEM((S // blk, HEAD_DIM, blk), BF16)],
        compiler_params=_params(2),
        name="fox_attn_online" if online_max else "fox_attn",
    )(qk_t, qk_t, vg_t, vg_t)


def _out_proj_kernel(o_ref, w_ref, x_ref, y_ref):
    y_t = jnp.dot(w_ref[...], o_ref[...], preferred_element_type=F32)
    y_ref[...] = x_ref[...] + y_t.T


def _out_proj(o_t, w_out_t, h):
    B, S, D = h.shape
    tm = _tile(S, 512)
    return pl.pallas_call(
        _out_proj_kernel,
        grid=(B, S // tm),
        in_specs=[
            pl.BlockSpec((None, D, tm), lambda b, s: (b, 0, s)),
            pl.BlockSpec((D, D), lambda b, s: (0, 0)),
            pl.BlockSpec((None, tm, D), lambda b, s: (b, s, 0)),
        ],
        out_specs=pl.BlockSpec((None, tm, D), lambda b, s: (b, s, 0)),
        out_shape=jax.ShapeDtypeStruct((B, S, D), F32),
        compiler_params=_params(2),
        name="out_proj",
    )(o_t, w_out_t, h)


def _fox_layer(h, g, w_in, b_f, gq, gk, w_out):
    B, S, D = h.shape
    H = D // HEAD_DIM
    w_qk_t = w_in[:, :2 * D].T.astype(BF16)
    w_vg_t = w_in[:, 2 * D:4 * D].T.astype(BF16)
    w_f_t = w_in[:, 4 * D:].T.astype(BF16)
    bound = (HEAD_DIM * ATTN_SCALE * BOUND_SLACK) * jnp.max(jnp.abs(gq)) * jnp.max(jnp.abs(gk))
    shifted = bound <= MAX_SHIFT
    shift = jnp.where(shifted, bound, 0.0).astype(F32)
    qk_t = _qk_proj(h, g, w_qk_t, w_f_t, b_f, gq, gk, shift)
    vg_t = _vg_proj(h, g, w_vg_t)
    o_t = lax.cond(shifted,
                   functools.partial(_attention, n_heads=H, online_max=False),
                   functools.partial(_attention, n_heads=H, online_max=True),
                   qk_t, vg_t)
    return _out_proj(o_t, w_out.T.astype(BF16), h)


def kernel(x, attn_norm_g, ffn_norm_g, final_norm_g, pool_w, pool_scale, fox_w_in, fox_b_f,
           fox_q_norm_g, fox_k_norm_g, fox_w_out, ffn_w_gate_up, ffn_w_down):
    B, S, D = x.shape
    depth = attn_norm_g.shape[0]
    h = x
    for i in range(depth):
        j = i // 2
        if i % 2 == 0:
            h = _pool_layer(h, attn_norm_g[i], pool_w[j].astype(BF16), pool_scale[j])
        else:
            h = _fox_layer(h, attn_norm_g[i], fox_w_in[j], fox_b_f[j], fox_q_norm_g[j],
                           fox_k_norm_g[j], fox_w_out[j])
        final_g = final_norm_g if i == depth - 1 else None
        h = _ffn(h.reshape(B * S, D), ffn_norm_g[i], ffn_w_gate_up[i].astype(BF16),
                 ffn_w_down[i].astype(BF16), final_g).reshape(B, S, D)
    return h
```

```python
import functools

import jax
import jax.numpy as jnp
from jax import lax
from jax.experimental import pallas as pl
from jax.experimental.pallas import tpu as pltpu

HEAD_DIM = 64
POOL_WINDOWS = (2, 4, 8, 16)
RMS_EPS = 1e-6
ATTN_SCALE = HEAD_DIM ** -0.5

POOL_HIST = 128
HEAD_ROWS = 128
AUG_ROWS = 16
HEADS_PER_STEP = 16
MAX_SHIFT = 30.0
BOUND_SLACK = 1.0 + 2.0 ** -6
MASKED_LOGIT = -1e30

V7X_VMEM_LIMIT = 60000 * 1024

FFN_TOKENS = 1024
FFN_HIDDEN = 512
POOL_TOKENS = 1024
PROJ_TOKENS = 1024
PROJ_ROWS = 1024
ATTN_QUERIES = 2048
ATTN_KEYS = 2048
ATTN_DIAG = 512
ATTN_ONLINE_BLOCK = 512
OUT_TOKENS = 512

F32 = jnp.float32
BF16 = jnp.bfloat16


def _params(n_axes):
    return pltpu.CompilerParams(
        dimension_semantics=("arbitrary",) * n_axes,
        vmem_limit_bytes=V7X_VMEM_LIMIT,
    )


def _tile(n, want):
    t = min(n, want)
    assert n % t == 0, (n, want)
    return t


def _rms(x, g):
    ms = jnp.mean(x * x, axis=-1, keepdims=True)
    return x * lax.rsqrt(ms + RMS_EPS) * g


def _split3(x):
    x1 = x.astype(BF16).astype(F32)
    r = x - x1
    x2 = r.astype(BF16).astype(F32)
    x3 = (r - x2).astype(BF16).astype(F32)
    return x1, x2, x3


def _ffn_kernel(x_ref, g_ref, wg_ref, wu_ref, wd_ref, *rest, final_norm):
    if final_norm:
        fg_ref, o_ref, xn_ref = rest
    else:
        o_ref, xn_ref = rest
    j = pl.program_id(1)

    @pl.when(j == 0)
    def _():
        x = x_ref[...]
        xn_ref[...] = _rms(x, g_ref[...]).astype(BF16)
        o_ref[...] = x

    xn = xn_ref[...]
    gate = jnp.dot(xn, wg_ref[...], preferred_element_type=F32)
    up = jnp.dot(xn, wu_ref[...], preferred_element_type=F32)
    a = (gate * jax.nn.sigmoid(gate) * up).astype(BF16)
    o_ref[...] += jnp.dot(a, wd_ref[...], preferred_element_type=F32)

    if final_norm:
        @pl.when(j == pl.num_programs(1) - 1)
        def _():
            o_ref[...] = _rms(o_ref[...], fg_ref[...])


def _ffn(h, g, w_gu, w_d, final_g=None):
    M, D = h.shape
    F = w_d.shape[0]
    tm = _tile(M, FFN_TOKENS)
    tf = _tile(F, FFN_HIDDEN)
    nf = F // tf
    final_norm = final_g is not None
    in_specs = [
        pl.BlockSpec((tm, D), lambda i, j: (i, 0)),
        pl.BlockSpec((1, D), lambda i, j: (0, 0)),
        pl.BlockSpec((D, tf), lambda i, j: (0, j)),
        pl.BlockSpec((D, tf), lambda i, j: (0, j + nf)),
        pl.BlockSpec((tf, D), lambda i, j: (j, 0)),
    ]
    args = [h, g.reshape(1, D), w_gu, w_gu, w_d]
    if final_norm:
        in_specs.append(pl.BlockSpec((1, D), lambda i, j: (0, 0)))
        args.append(final_g.reshape(1, D))
    return pl.pallas_call(
        functools.partial(_ffn_kernel, final_norm=final_norm),
        grid=(M // tm, nf),
        in_specs=in_specs,
        out_specs=pl.BlockSpec((tm, D), lambda i, j: (i, 0)),
        out_shape=jax.ShapeDtypeStruct((M, D), F32),
        scratch_shapes=[pltpu.VMEM((tm, D), BF16)],
        compiler_params=_params(2),
        name="ffn",
    )(*args)


def _pool_kernel(x_ref, g_ref, band_ref, cnt_ref, w_ref, sc_ref, o_ref, hi_ref, lo_ref, *,
                 n_groups):
    si = pl.program_id(1)
    tm, D = x_ref.shape
    C = D // n_groups

    @pl.when(si == 0)
    def _():
        hi_ref[:POOL_HIST, :] = jnp.zeros((POOL_HIST, D), BF16)
        lo_ref[:POOL_HIST, :] = jnp.zeros((POOL_HIST, D), BF16)

    x = x_ref[...]
    hn = _rms(x, g_ref[...])
    hi = hn.astype(BF16)
    hi_ref[POOL_HIST:, :] = hi
    lo_ref[POOL_HIST:, :] = (hn - hi.astype(F32)).astype(BF16)

    for gi in range(n_groups):
        cols = slice(gi * C, (gi + 1) * C)
        band = band_ref[gi]
        slabs = []
        for r0 in range(0, tm, POOL_HIST):
            rows = slice(r0, r0 + 2 * POOL_HIST)
            slabs.append(jnp.dot(band, hi_ref[rows, cols], preferred_element_type=F32)
                         + jnp.dot(band, lo_ref[rows, cols], preferred_element_type=F32))
        win = jnp.concatenate(slabs, axis=0)
        feat = win / cnt_ref[:, gi:gi + 1] - hn[:, cols]
        y = jnp.dot(feat.astype(BF16), w_ref[gi], preferred_element_type=F32)
        o_ref[:, cols] = x[:, cols] + y * sc_ref[:, cols]

    hi_ref[:POOL_HIST, :] = hi_ref[tm:, :]
    lo_ref[:POOL_HIST, :] = lo_ref[tm:, :]


def _pool_layer(h, g, w_group, scale):
    B, S, D = h.shape
    G = len(POOL_WINDOWS)
    C = D // G
    tm = _tile(S, POOL_TOKENS)
    assert tm % POOL_HIST == 0 and POOL_HIST >= max(POOL_WINDOWS)
    r = jnp.arange(POOL_HIST)[:, None] + POOL_HIST
    c = jnp.arange(2 * POOL_HIST)[None, :]
    band = jnp.stack([((c <= r) & (c > r - w)) for w in POOL_WINDOWS]).astype(BF16)
    n_valid = jnp.arange(1, S + 1, dtype=F32)[:, None]
    cnt = jnp.minimum(n_valid, jnp.asarray(POOL_WINDOWS, F32)[None, :])
    return pl.pallas_call(
        functools.partial(_pool_kernel, n_groups=G),
        grid=(B, S // tm),
        in_specs=[
            pl.BlockSpec((None, tm, D), lambda b, s: (b, s, 0)),
            pl.BlockSpec((1, D), lambda b, s: (0, 0)),
            pl.BlockSpec((G, POOL_HIST, 2 * POOL_HIST), lambda b, s: (0, 0, 0)),
            pl.BlockSpec((tm, G), lambda b, s: (s, 0)),
            pl.BlockSpec((G, C, C), lambda b, s: (0, 0, 0)),
            pl.BlockSpec((1, D), lambda b, s: (0, 0)),
        ],
        out_specs=pl.BlockSpec((None, tm, D), lambda b, s: (b, s, 0)),
        out_shape=jax.ShapeDtypeStruct((B, S, D), F32),
        scratch_shapes=[
            pltpu.VMEM((POOL_HIST + tm, D), BF16), pltpu.VMEM((POOL_HIST + tm, D), BF16),
        ],
        compiler_params=_params(2),
        name="pool",
    )(h, g.reshape(1, D), band, cnt, w_group, scale.reshape(1, D))


def _qk_proj_kernel(x_ref, g_ref, w_ref, wf_ref, bf_ref, tri_ref, gq_ref, gk_ref, shift_ref,
                    o_ref, xn_ref, c_ref, carry_ref, *, n_q_steps):
    si = pl.program_id(1)
    j = pl.program_id(2)
    tm = x_ref.shape[0]

    @pl.when(j == 0)
    def _():
        xn = _rms(x_ref[...], g_ref[...]).astype(BF16)
        xn_ref[...] = xn
        z = lax.dot_general(wf_ref[...], xn, (((1,), (1,)), ((), ())),
                            preferred_element_type=F32) + bf_ref[...]
        logf = jnp.minimum(z, 0.0) - jnp.log1p(jnp.exp(-jnp.abs(z)))
        f1, f2, f3 = _split3(logf)
        tri = tri_ref[...]
        csum = (jnp.dot(f1.astype(BF16), tri, preferred_element_type=F32)
                + jnp.dot(f2.astype(BF16), tri, preferred_element_type=F32)
                + jnp.dot(f3.astype(BF16), tri, preferred_element_type=F32))
        prev = jnp.where(si == 0, 0.0, carry_ref[...])
        csum = csum + prev
        carry_ref[...] = csum[:, tm - 1:tm]
        for side, val in ((0, csum - shift_ref[...]), (1, -csum)):
            for part, term in enumerate(_split3(val)):
                c_ref[side, part] = term

    y = lax.dot_general(w_ref[...], xn_ref[...], (((1,), (1,)), ((), ())),
                        preferred_element_type=F32)
    heads = w_ref.shape[0] // HEAD_DIM
    is_q = j < n_q_steps
    gain = jnp.where(is_q, gq_ref[...] * ATTN_SCALE, gk_ref[...])
    side = jnp.where(is_q, 0, 1)
    c_row0 = jnp.where(is_q, 0, 3)
    head0 = jnp.where(is_q, j, j - n_q_steps) * heads
    row = lax.broadcasted_iota(jnp.int32, (AUG_ROWS, tm), 0)
    for hh in range(heads):
        yh = y[hh * HEAD_DIM:(hh + 1) * HEAD_DIM]
        ms = jnp.mean(yh * yh, axis=0, keepdims=True)
        yn = yh * lax.rsqrt(ms + RMS_EPS) * gain
        hd = head0 + hh
        c1 = c_ref[side, 0, pl.ds(hd, 1), :]
        c2 = c_ref[side, 1, pl.ds(hd, 1), :]
        c3 = c_ref[side, 2, pl.ds(hd, 1), :]
        sel = row - c_row0
        in_c = (sel >= 0) & (sel < 3)
        in_one = (row >= 3 - c_row0) & (row < 6 - c_row0)
        cval = jnp.where(sel == 0, c1, jnp.where(sel == 1, c2, c3))
        aug = jnp.where(in_c, cval, jnp.where(in_one, 1.0, 0.0))
        base = hh * HEAD_ROWS
        o_ref[base:base + HEAD_DIM, :] = yn.astype(BF16)
        o_ref[base + HEAD_DIM:base + HEAD_DIM + AUG_ROWS, :] = aug.astype(BF16)
        o_ref[base + HEAD_DIM + AUG_ROWS:base + HEAD_ROWS, :] = jnp.zeros(
            (HEAD_ROWS - HEAD_DIM - AUG_ROWS, tm), BF16)


def _qk_proj(h, g, w_qk_t, w_f_t, b_f, gq, gk, shift):
    B, S, D = h.shape
    H = D // HEAD_DIM
    tm = _tile(S, PROJ_TOKENS)
    heads = _tile(H, HEADS_PER_STEP)
    rows = heads * HEAD_DIM
    n_q_steps = D // rows
    tri = (jnp.arange(tm)[:, None] <= jnp.arange(tm)[None, :]).astype(BF16)
    return pl.pallas_call(
        functools.partial(_qk_proj_kernel, n_q_steps=n_q_steps),
        grid=(B, S // tm, 2 * n_q_steps),
        in_specs=[
            pl.BlockSpec((None, tm, D), lambda b, s, j: (b, s, 0)),
            pl.BlockSpec((1, D), lambda b, s, j: (0, 0)),
            pl.BlockSpec((rows, D), lambda b, s, j: (j, 0)),
            pl.BlockSpec((H, D), lambda b, s, j: (0, 0)),
            pl.BlockSpec((H, 1), lambda b, s, j: (0, 0)),
            pl.BlockSpec((tm, tm), lambda b, s, j: (0, 0)),
            pl.BlockSpec((HEAD_DIM, 1), lambda b, s, j: (0, 0)),
            pl.BlockSpec((HEAD_DIM, 1), lambda b, s, j: (0, 0)),
            pl.BlockSpec((1, 1), lambda b, s, j: (0, 0)),
        ],
        out_specs=pl.BlockSpec((None, heads * HEAD_ROWS, tm), lambda b, s, j: (b, j, s)),
        out_shape=jax.ShapeDtypeStruct((B, 2 * H * HEAD_ROWS, S), BF16),
        scratch_shapes=[
            pltpu.VMEM((tm, D), BF16),
            pltpu.VMEM((2, 3, H, tm), F32),
            pltpu.VMEM((H, 1), F32),
        ],
        compiler_params=_params(3),
        name="qk_proj",
    )(h, g.reshape(1, D), w_qk_t, w_f_t, b_f.reshape(H, 1), tri,
      gq.reshape(HEAD_DIM, 1), gk.reshape(HEAD_DIM, 1), shift.reshape(1, 1))


def _vg_proj_kernel(x_ref, g_ref, w_ref, o_ref, xn_ref, *, n_v_steps):
    j = pl.program_id(2)

    @pl.when(j == 0)
    def _():
        xn_ref[...] = _rms(x_ref[...], g_ref[...]).astype(BF16)

    y = lax.dot_general(w_ref[...], xn_ref[...], (((1,), (1,)), ((), ())),
                        preferred_element_type=F32)

    @pl.when(j < n_v_steps)
    def _():
        o_ref[...] = y.astype(BF16)

    @pl.when(j >= n_v_steps)
    def _():
        o_ref[...] = jax.nn.sigmoid(y).astype(BF16)


def _vg_proj(h, g, w_vg_t):
    B, S, D = h.shape
    tm = _tile(S, PROJ_TOKENS)
    tn = _tile(D, PROJ_ROWS)
    n_v_steps = D // tn
    return pl.pallas_call(
        functools.partial(_vg_proj_kernel, n_v_steps=n_v_steps),
        grid=(B, S // tm, 2 * n_v_steps),
        in_specs=[
            pl.BlockSpec((None, tm, D), lambda b, s, j: (b, s, 0)),
            pl.BlockSpec((1, D), lambda b, s, j: (0, 0)),
            pl.BlockSpec((tn, D), lambda b, s, j: (j, 0)),
        ],
        out_specs=pl.BlockSpec((None, tn, tm), lambda b, s, j: (b, j, s)),
        out_shape=jax.ShapeDtypeStruct((B, 2 * D, S), BF16),
        scratch_shapes=[pltpu.VMEM((tm, D), BF16)],
        compiler_params=_params(3),
        name="vg_proj",
    )(h, g.reshape(1, D), w_vg_t)


def _attn_shifted_kernel(q_ref, k_ref, v_ref, gate_ref, o_ref, kn_ref, *, tq, tk, td):
    S = q_ref.shape[1]
    n_chunks = tq // td
    for c in range(S // td):
        cs = slice(c * td, (c + 1) * td)
        kn_ref[cs, :] = k_ref[:, cs].astype(F32).T.astype(BF16)

    kk = lax.broadcasted_iota(jnp.int32, (td, td), 0)
    qq = lax.broadcasted_iota(jnp.int32, (td, td), 1)
    causal = kk <= qq

    def probs(k_rows, q_cols, v_cols, mask):
        s_t = jnp.dot(kn_ref[k_rows, :], q_cols, preferred_element_type=F32)
        if mask:
            s_t = jnp.where(causal, s_t, MASKED_LOGIT)
        p = jnp.exp(s_t)
        pv = jnp.dot(v_ref[:, v_cols], p.astype(BF16), preferred_element_type=F32)
        return p, pv

    def row_sums8(p):
        return jnp.sum(p.reshape(p.shape[0] // 8, 8, p.shape[1]), axis=0)

    for qi in range(S // tq):
        q0 = qi * tq
        q_t = q_ref[:, q0:q0 + tq]
        acc = [jnp.zeros((HEAD_DIM, td), F32) for _ in range(n_chunks)]
        l8 = [jnp.zeros((8, td), F32) for _ in range(n_chunks)]
        for kj in range(q0 // tk):
            ks = slice(kj * tk, (kj + 1) * tk)
            p, pv = probs(ks, q_t, ks, mask=False)
            for c in range(n_chunks):
                cc = slice(c * td, (c + 1) * td)
                acc[c] = acc[c] + pv[:, cc]
                l8[c] = l8[c] + row_sums8(p[:, cc])
        for d in range(n_chunks):
            ks = slice(q0 + d * td, q0 + (d + 1) * td)
            p, pv = probs(ks, q_t[:, d * td:(d + 1) * td], ks, mask=True)
            acc[d] = acc[d] + pv
            l8[d] = l8[d] + row_sums8(p)
            if d + 1 < n_chunks:
                p, pv = probs(ks, q_t[:, (d + 1) * td:], ks, mask=False)
                for c in range(d + 1, n_chunks):
                    cc = slice((c - d - 1) * td, (c - d) * td)
                    acc[c] = acc[c] + pv[:, cc]
                    l8[c] = l8[c] + row_sums8(p[:, cc])
        for c in range(n_chunks):
            cs = slice(q0 + c * td, q0 + (c + 1) * td)
            l = jnp.sum(l8[c], axis=0, keepdims=True)
            o_ref[:, cs] = (acc[c] / l * gate_ref[:, cs].astype(F32)).astype(BF16)


def _attn_online_kernel(q_ref, k_ref, v_ref, gate_ref, o_ref, kn_ref, vb_ref, *, blk):
    S = q_ref.shape[1]
    n_blk = S // blk

    for c in range(n_blk):
        cs = slice(c * blk, (c + 1) * blk)
        kn_ref[c] = k_ref[:, cs].astype(F32).T.astype(BF16)
        vb_ref[c] = v_ref[:, cs]

    kk = lax.broadcasted_iota(jnp.int32, (blk, blk), 0)
    qq = lax.broadcasted_iota(jnp.int32, (blk, blk), 1)
    causal = kk <= qq

    def step(q_t, k_blk, v_t, carry, mask):
        m, l, acc = carry
        s_t = jnp.dot(k_blk, q_t, preferred_element_type=F32)
        if mask:
            s_t = jnp.where(causal, s_t, -jnp.inf)
        m_new = jnp.maximum(m, jnp.max(s_t, axis=0, keepdims=True))
        alpha = jnp.exp(m - m_new)
        p = jnp.exp(s_t - m_new)
        l = alpha * l + jnp.sum(p, axis=0, keepdims=True)
        acc = alpha * acc + jnp.dot(v_t, p.astype(BF16), preferred_element_type=F32)
        return m_new, l, acc

    for qi in range(n_blk):
        qs = slice(qi * blk, (qi + 1) * blk)
        q_t = q_ref[:, qs]
        carry = (jnp.full((1, blk), -jnp.inf, F32), jnp.zeros((1, blk), F32),
                 jnp.zeros((HEAD_DIM, blk), F32))

        def body(kj, carry, q_t=q_t):
            return step(q_t, kn_ref[kj], vb_ref[kj], carry, mask=False)

        carry = lax.fori_loop(0, qi, body, carry)
        m, l, acc = step(q_t, kn_ref[qi], vb_ref[qi], carry, mask=True)
        o_ref[:, qs] = (acc / l * gate_ref[:, qs].astype(F32)).astype(BF16)


def _attention(qk_t, vg_t, n_heads, online_max):
    B, _, S = qk_t.shape
    D = n_heads * HEAD_DIM
    if online_max:
        blk = _tile(S, ATTN_ONLINE_BLOCK)
        body = functools.partial(_attn_online_kernel, blk=blk)
        scratch = [pltpu.VMEM((S // blk, blk, HEAD_ROWS), BF16),
                   pltpu.VMEM((S // blk, HEAD_DIM, blk), BF16)]
    else:
        tq = _tile(S, ATTN_QUERIES)
        tk = _tile(tq, ATTN_KEYS)
        td = _tile(tk, ATTN_DIAG)
        body = functools.partial(_attn_shifted_kernel, tq=tq, tk=tk, td=td)
        scratch = [pltpu.VMEM((S, HEAD_ROWS), BF16)]
    return pl.pallas_call(
        body,
        grid=(B, n_heads),
        in_specs=[
            pl.BlockSpec((None, HEAD_ROWS, S), lambda b, h: (b, h, 0)),
            pl.BlockSpec((None, HEAD_ROWS, S), lambda b, h: (b, h + n_heads, 0)),
            pl.BlockSpec((None, HEAD_DIM, S), lambda b, h: (b, h, 0)),
            pl.BlockSpec((None, HEAD_DIM, S), lambda b, h: (b, h + n_heads, 0)),
        ],
        out_specs=pl.BlockSpec((None, HEAD_DIM, S), lambda b, h: (b, h, 0)),
        out_shape=jax.ShapeDtypeStruct((B, D, S), BF16),
        scratch_shapes=scratch,
        compiler_params=_params(2),
        name="fox_attn_online" if online_max else "fox_attn",
    )(qk_t, qk_t, vg_t, vg_t)


def _out_proj_kernel(o_ref, w_ref, x_ref, y_ref):
    y_t = jnp.dot(w_ref[...], o_ref[...], preferred_element_type=F32)
    y_ref[...] = x_ref[...] + y_t.T


def _out_proj(o_t, w_out_t, h):
    B, S, D = h.shape
    tm = _tile(S, OUT_TOKENS)
    return pl.pallas_call(
        _out_proj_kernel,
        grid=(B, S // tm),
        in_specs=[
            pl.BlockSpec((None, D, tm), lambda b, s: (b, 0, s)),
            pl.BlockSpec((D, D), lambda b, s: (0, 0)),
            pl.BlockSpec((None, tm, D), lambda b, s: (b, s, 0)),
        ],
        out_specs=pl.BlockSpec((None, tm, D), lambda b, s: (b, s, 0)),
        out_shape=jax.ShapeDtypeStruct((B, S, D), F32),
        compiler_params=_params(2),
        name="out_proj",
    )(o_t, w_out_t, h)


def _fox_layer(h, g, w_in, b_f, gq, gk, w_out):
    B, S, D = h.shape
    H = D // HEAD_DIM
    w_qk_t = w_in[:, :2 * D].T.astype(BF16)
    w_vg_t = w_in[:, 2 * D:4 * D].T.astype(BF16)
    w_f_t = w_in[:, 4 * D:].T.astype(BF16)
    bound = (HEAD_DIM * ATTN_SCALE * BOUND_SLACK) * jnp.max(jnp.abs(gq)) * jnp.max(jnp.abs(gk))
    shifted = bound <= MAX_SHIFT
    shift = jnp.where(shifted, bound, 0.0).astype(F32)
    qk_t = _qk_proj(h, g, w_qk_t, w_f_t, b_f, gq, gk, shift)
    vg_t = _vg_proj(h, g, w_vg_t)
    o_t = lax.cond(shifted,
                   functools.partial(_attention, n_heads=H, online_max=False),
                   functools.partial(_attention, n_heads=H, online_max=True),
                   qk_t, vg_t)
    return _out_proj(o_t, w_out.T.astype(BF16), h)


def kernel(x, attn_norm_g, ffn_norm_g, final_norm_g, pool_w, pool_scale, fox_w_in, fox_b_f,
           fox_q_norm_g, fox_k_norm_g, fox_w_out, ffn_w_gate_up, ffn_w_down):
    B, S, D = x.shape
    depth = attn_norm_g.shape[0]
    h = x
    for i in range(depth):
        j = i // 2
        if i % 2 == 0:
            h = _pool_layer(h, attn_norm_g[i], pool_w[j].astype(BF16), pool_scale[j])
        else:
            h = _fox_layer(h, attn_norm_g[i], fox_w_in[j], fox_b_f[j], fox_q_norm_g[j],
                           fox_k_norm_g[j], fox_w_out[j])
        final_g = final_norm_g if i == depth - 1 else None
        h = _ffn(h.reshape(B * S, D), ffn_norm_g[i], ffn_w_gate_up[i].astype(BF16),
                 ffn_w_down[i].astype(BF16), final_g).reshape(B, S, D)
    return h
```

```python
import functools

import jax
import jax.numpy as jnp
from jax import lax
from jax.experimental import pallas as pl
from jax.experimental.pallas import tpu as pltpu

HEAD_DIM = 64
POOL_WINDOWS = (2, 4, 8, 16)
RMS_EPS = 1e-6
ATTN_SCALE = HEAD_DIM ** -0.5

POOL_HIST = 128
HEAD_ROWS = 128
AUG_ROWS = 16
HEADS_PER_STEP = 16
MAX_SHIFT = 30.0
BOUND_SLACK = 1.0 + 2.0 ** -6
MASKED_LOGIT = -1e30

V7X_VMEM_LIMIT = 60000 * 1024

FFN_TOKENS = 1024
FFN_HIDDEN = 512
POOL_TOKENS = 1024
PROJ_TOKENS = 1024
PROJ_ROWS = 1024
ATTN_QUERIES = 2048
ATTN_KEYS = 2048
ATTN_DIAG = 512
ATTN_ONLINE_BLOCK = 512
OUT_TOKENS = 512
WEIGHT_COLS = 512

F32 = jnp.float32
BF16 = jnp.bfloat16


def _params(n_axes):
    return pltpu.CompilerParams(
        dimension_semantics=("arbitrary",) * n_axes,
        vmem_limit_bytes=V7X_VMEM_LIMIT,
    )


def _tile(n, want):
    t = min(n, want)
    assert n % t == 0, (n, want)
    return t


def _rms(x, g):
    ms = jnp.mean(x * x, axis=-1, keepdims=True)
    return x * lax.rsqrt(ms + RMS_EPS) * g


def _split3(x):
    x1 = x.astype(BF16).astype(F32)
    r = x - x1
    x2 = r.astype(BF16).astype(F32)
    x3 = (r - x2).astype(BF16).astype(F32)
    return x1, x2, x3


def _ffn_kernel(x_ref, g_ref, wg_ref, wu_ref, wd_ref, *rest, final_norm):
    if final_norm:
        fg_ref, o_ref, xn_ref = rest
    else:
        o_ref, xn_ref = rest
    j = pl.program_id(1)

    @pl.when(j == 0)
    def _():
        x = x_ref[...]
        xn_ref[...] = _rms(x, g_ref[...]).astype(BF16)
        o_ref[...] = x

    xn = xn_ref[...]
    gate = jnp.dot(xn, wg_ref[...], preferred_element_type=F32)
    up = jnp.dot(xn, wu_ref[...], preferred_element_type=F32)
    a = (gate * jax.nn.sigmoid(gate) * up).astype(BF16)
    o_ref[...] += jnp.dot(a, wd_ref[...], preferred_element_type=F32)

    if final_norm:
        @pl.when(j == pl.num_programs(1) - 1)
        def _():
            o_ref[...] = _rms(o_ref[...], fg_ref[...])


def _ffn(h, g, w_gu, w_d, layer, final_g=None):
    M, D = h.shape
    F = w_d.shape[1]
    tm = _tile(M, FFN_TOKENS)
    tf = _tile(F, FFN_HIDDEN)
    nf = F // tf
    final_norm = final_g is not None
    in_specs = [
        pl.BlockSpec((tm, D), lambda i, j: (i, 0)),
        pl.BlockSpec((1, D), lambda i, j: (0, 0)),
        pl.BlockSpec((None, D, tf), lambda i, j: (layer, 0, j)),
        pl.BlockSpec((None, D, tf), lambda i, j: (layer, 0, j + nf)),
        pl.BlockSpec((None, tf, D), lambda i, j: (layer, j, 0)),
    ]
    args = [h, g.reshape(1, D), w_gu, w_gu, w_d]
    if final_norm:
        in_specs.append(pl.BlockSpec((1, D), lambda i, j: (0, 0)))
        args.append(final_g.reshape(1, D))
    return pl.pallas_call(
        functools.partial(_ffn_kernel, final_norm=final_norm),
        grid=(M // tm, nf),
        in_specs=in_specs,
        out_specs=pl.BlockSpec((tm, D), lambda i, j: (i, 0)),
        out_shape=jax.ShapeDtypeStruct((M, D), F32),
        scratch_shapes=[pltpu.VMEM((tm, D), BF16)],
        compiler_params=_params(2),
        name="ffn",
    )(*args)


def _pool_kernel(x_ref, g_ref, band_ref, cnt_ref, w_ref, sc_ref, o_ref, hi_ref, lo_ref, *,
                 n_groups):
    si = pl.program_id(1)
    tm, D = x_ref.shape
    C = D // n_groups

    @pl.when(si == 0)
    def _():
        hi_ref[:POOL_HIST, :] = jnp.zeros((POOL_HIST, D), BF16)
        lo_ref[:POOL_HIST, :] = jnp.zeros((POOL_HIST, D), BF16)

    x = x_ref[...]
    hn = _rms(x, g_ref[...])
    hi = hn.astype(BF16)
    hi_ref[POOL_HIST:, :] = hi
    lo_ref[POOL_HIST:, :] = (hn - hi.astype(F32)).astype(BF16)

    for gi in range(n_groups):
        cols = slice(gi * C, (gi + 1) * C)
        band = band_ref[gi]
        slabs = []
        for r0 in range(0, tm, POOL_HIST):
            rows = slice(r0, r0 + 2 * POOL_HIST)
            slabs.append(jnp.dot(band, hi_ref[rows, cols], preferred_element_type=F32)
                         + jnp.dot(band, lo_ref[rows, cols], preferred_element_type=F32))
        win = jnp.concatenate(slabs, axis=0)
        feat = win / cnt_ref[:, gi:gi + 1] - hn[:, cols]
        y = jnp.dot(feat.astype(BF16), w_ref[gi], preferred_element_type=F32)
        o_ref[:, cols] = x[:, cols] + y * sc_ref[:, cols]

    hi_ref[:POOL_HIST, :] = hi_ref[tm:, :]
    lo_ref[:POOL_HIST, :] = lo_ref[tm:, :]


def _pool_layer(h, g, w_group, layer, scale):
    B, S, D = h.shape
    G = len(POOL_WINDOWS)
    C = D // G
    tm = _tile(S, POOL_TOKENS)
    assert tm % POOL_HIST == 0 and POOL_HIST >= max(POOL_WINDOWS)
    r = jnp.arange(POOL_HIST)[:, None] + POOL_HIST
    c = jnp.arange(2 * POOL_HIST)[None, :]
    band = jnp.stack([((c <= r) & (c > r - w)) for w in POOL_WINDOWS]).astype(BF16)
    n_valid = jnp.arange(1, S + 1, dtype=F32)[:, None]
    cnt = jnp.minimum(n_valid, jnp.asarray(POOL_WINDOWS, F32)[None, :])
    return pl.pallas_call(
        functools.partial(_pool_kernel, n_groups=G),
        grid=(B, S // tm),
        in_specs=[
            pl.BlockSpec((None, tm, D), lambda b, s: (b, s, 0)),
            pl.BlockSpec((1, D), lambda b, s: (0, 0)),
            pl.BlockSpec((G, POOL_HIST, 2 * POOL_HIST), lambda b, s: (0, 0, 0)),
            pl.BlockSpec((tm, G), lambda b, s: (s, 0)),
            pl.BlockSpec((None, G, C, C), lambda b, s: (layer, 0, 0, 0)),
            pl.BlockSpec((1, D), lambda b, s: (0, 0)),
        ],
        out_specs=pl.BlockSpec((None, tm, D), lambda b, s: (b, s, 0)),
        out_shape=jax.ShapeDtypeStruct((B, S, D), F32),
        scratch_shapes=[
            pltpu.VMEM((POOL_HIST + tm, D), BF16), pltpu.VMEM((POOL_HIST + tm, D), BF16),
        ],
        compiler_params=_params(2),
        name="pool",
    )(h, g.reshape(1, D), band, cnt, w_group, scale.reshape(1, D))


def _qk_proj_kernel(x_ref, g_ref, w_ref, wf_ref, bf_ref, tri_ref, gq_ref, gk_ref, shift_ref,
                    o_ref, xn_ref, c_ref, carry_ref, *, n_q_steps):
    si = pl.program_id(1)
    j = pl.program_id(2)
    tm = x_ref.shape[0]

    @pl.when(j == 0)
    def _():
        xn = _rms(x_ref[...], g_ref[...]).astype(BF16)
        xn_ref[...] = xn
        z = lax.dot_general(wf_ref[...], xn, (((1,), (1,)), ((), ())),
                            preferred_element_type=F32) + bf_ref[...]
        logf = jnp.minimum(z, 0.0) - jnp.log1p(jnp.exp(-jnp.abs(z)))
        f1, f2, f3 = _split3(logf)
        tri = tri_ref[...]
        csum = (jnp.dot(f1.astype(BF16), tri, preferred_element_type=F32)
                + jnp.dot(f2.astype(BF16), tri, preferred_element_type=F32)
                + jnp.dot(f3.astype(BF16), tri, preferred_element_type=F32))
        prev = jnp.where(si == 0, 0.0, carry_ref[...])
        csum = csum + prev
        carry_ref[...] = csum[:, tm - 1:tm]
        for side, val in ((0, csum - shift_ref[...]), (1, -csum)):
            for part, term in enumerate(_split3(val)):
                c_ref[side, part] = term

    y = lax.dot_general(w_ref[...], xn_ref[...], (((1,), (1,)), ((), ())),
                        preferred_element_type=F32)
    heads = w_ref.shape[0] // HEAD_DIM
    is_q = j < n_q_steps
    gain = jnp.where(is_q, gq_ref[...] * ATTN_SCALE, gk_ref[...])
    side = jnp.where(is_q, 0, 1)
    c_row0 = jnp.where(is_q, 0, 3)
    head0 = jnp.where(is_q, j, j - n_q_steps) * heads
    row = lax.broadcasted_iota(jnp.int32, (AUG_ROWS, tm), 0)
    for hh in range(heads):
        yh = y[hh * HEAD_DIM:(hh + 1) * HEAD_DIM]
        ms = jnp.mean(yh * yh, axis=0, keepdims=True)
        yn = yh * lax.rsqrt(ms + RMS_EPS) * gain
        hd = head0 + hh
        c1 = c_ref[side, 0, pl.ds(hd, 1), :]
        c2 = c_ref[side, 1, pl.ds(hd, 1), :]
        c3 = c_ref[side, 2, pl.ds(hd, 1), :]
        sel = row - c_row0
        in_c = (sel >= 0) & (sel < 3)
        in_one = (row >= 3 - c_row0) & (row < 6 - c_row0)
        cval = jnp.where(sel == 0, c1, jnp.where(sel == 1, c2, c3))
        aug = jnp.where(in_c, cval, jnp.where(in_one, 1.0, 0.0))
        base = hh * HEAD_ROWS
        o_ref[base:base + HEAD_DIM, :] = yn.astype(BF16)
        o_ref[base + HEAD_DIM:base + HEAD_DIM + AUG_ROWS, :] = aug.astype(BF16)
        o_ref[base + HEAD_DIM + AUG_ROWS:base + HEAD_ROWS, :] = jnp.zeros(
            (HEAD_ROWS - HEAD_DIM - AUG_ROWS, tm), BF16)


def _qk_proj(h, g, w_in_t, layer, b_f, gq, gk, shift):
    B, S, D = h.shape
    H = D // HEAD_DIM
    f_block = 4 * D // H
    tm = _tile(S, PROJ_TOKENS)
    heads = _tile(H, HEADS_PER_STEP)
    rows = heads * HEAD_DIM
    n_q_steps = D // rows
    tri = (jnp.arange(tm)[:, None] <= jnp.arange(tm)[None, :]).astype(BF16)
    return pl.pallas_call(
        functools.partial(_qk_proj_kernel, n_q_steps=n_q_steps),
        grid=(B, S // tm, 2 * n_q_steps),
        in_specs=[
            pl.BlockSpec((None, tm, D), lambda b, s, j: (b, s, 0)),
            pl.BlockSpec((1, D), lambda b, s, j: (0, 0)),
            pl.BlockSpec((None, rows, D), lambda b, s, j: (layer, j, 0)),
            pl.BlockSpec((None, H, D), lambda b, s, j: (layer, f_block, 0)),
            pl.BlockSpec((H, 1), lambda b, s, j: (0, 0)),
            pl.BlockSpec((tm, tm), lambda b, s, j: (0, 0)),
            pl.BlockSpec((HEAD_DIM, 1), lambda b, s, j: (0, 0)),
            pl.BlockSpec((HEAD_DIM, 1), lambda b, s, j: (0, 0)),
            pl.BlockSpec((1, 1), lambda b, s, j: (0, 0)),
        ],
        out_specs=pl.BlockSpec((None, heads * HEAD_ROWS, tm), lambda b, s, j: (b, j, s)),
        out_shape=jax.ShapeDtypeStruct((B, 2 * H * HEAD_ROWS, S), BF16),
        scratch_shapes=[
            pltpu.VMEM((tm, D), BF16),
            pltpu.VMEM((2, 3, H, tm), F32),
            pltpu.VMEM((H, 1), F32),
        ],
        compiler_params=_params(3),
        name="qk_proj",
    )(h, g.reshape(1, D), w_in_t, w_in_t, b_f.reshape(H, 1), tri,
      gq.reshape(HEAD_DIM, 1), gk.reshape(HEAD_DIM, 1), shift.reshape(1, 1))


def _vg_proj_kernel(x_ref, g_ref, w_ref, o_ref, xn_ref, *, n_v_steps):
    j = pl.program_id(2)

    @pl.when(j == 0)
    def _():
        xn_ref[...] = _rms(x_ref[...], g_ref[...]).astype(BF16)

    y = lax.dot_general(w_ref[...], xn_ref[...], (((1,), (1,)), ((), ())),
                        preferred_element_type=F32)

    @pl.when(j < n_v_steps)
    def _():
        o_ref[...] = y.astype(BF16)

    @pl.when(j >= n_v_steps)
    def _():
        o_ref[...] = jax.nn.sigmoid(y).astype(BF16)


def _vg_proj(h, g, w_in_t, layer):
    B, S, D = h.shape
    tm = _tile(S, PROJ_TOKENS)
    tn = _tile(D, PROJ_ROWS)
    n_v_steps = D // tn
    row0 = 2 * n_v_steps
    return pl.pallas_call(
        functools.partial(_vg_proj_kernel, n_v_steps=n_v_steps),
        grid=(B, S // tm, 2 * n_v_steps),
        in_specs=[
            pl.BlockSpec((None, tm, D), lambda b, s, j: (b, s, 0)),
            pl.BlockSpec((1, D), lambda b, s, j: (0, 0)),
            pl.BlockSpec((None, tn, D), lambda b, s, j: (layer, row0 + j, 0)),
        ],
        out_specs=pl.BlockSpec((None, tn, tm), lambda b, s, j: (b, j, s)),
        out_shape=jax.ShapeDtypeStruct((B, 2 * D, S), BF16),
        scratch_shapes=[pltpu.VMEM((tm, D), BF16)],
        compiler_params=_params(3),
        name="vg_proj",
    )(h, g.reshape(1, D), w_in_t)


def _attn_shifted_kernel(q_ref, k_ref, v_ref, gate_ref, o_ref, kn_ref, *, tq, tk, td):
    S = q_ref.shape[1]
    n_chunks = tq // td
    for c in range(S // td):
        cs = slice(c * td, (c + 1) * td)
        kn_ref[cs, :] = k_ref[:, cs].astype(F32).T.astype(BF16)

    kk = lax.broadcasted_iota(jnp.int32, (td, td), 0)
    qq = lax.broadcasted_iota(jnp.int32, (td, td), 1)
    causal = kk <= qq

    def probs(k_rows, q_cols, v_cols, mask):
        s_t = jnp.dot(kn_ref[k_rows, :], q_cols, preferred_element_type=F32)
        if mask:
            s_t = jnp.where(causal, s_t, MASKED_LOGIT)
        p = jnp.exp(s_t)
        pv = jnp.dot(v_ref[:, v_cols], p.astype(BF16), preferred_element_type=F32)
        return p, pv

    def row_sums8(p):
        return jnp.sum(p.reshape(p.shape[0] // 8, 8, p.shape[1]), axis=0)

    for qi in range(S // tq):
        q0 = qi * tq
        q_t = q_ref[:, q0:q0 + tq]
        acc = [jnp.zeros((HEAD_DIM, td), F32) for _ in range(n_chunks)]
        l8 = [jnp.zeros((8, td), F32) for _ in range(n_chunks)]
        for kj in range(q0 // tk):
            ks = slice(kj * tk, (kj + 1) * tk)
            p, pv = probs(ks, q_t, ks, mask=False)
            for c in range(n_chunks):
                cc = slice(c * td, (c + 1) * td)
                acc[c] = acc[c] + pv[:, cc]
                l8[c] = l8[c] + row_sums8(p[:, cc])
        for d in range(n_chunks):
            ks = slice(q0 + d * td, q0 + (d + 1) * td)
            p, pv = probs(ks, q_t[:, d * td:(d + 1) * td], ks, mask=True)
            acc[d] = acc[d] + pv
            l8[d] = l8[d] + row_sums8(p)
            if d + 1 < n_chunks:
                p, pv = probs(ks, q_t[:, (d + 1) * td:], ks, mask=False)
                for c in range(d + 1, n_chunks):
                    cc = slice((c - d - 1) * td, (c - d) * td)
                    acc[c] = acc[c] + pv[:, cc]
                    l8[c] = l8[c] + row_sums8(p[:, cc])
        for c in range(n_chunks):
            cs = slice(q0 + c * td, q0 + (c + 1) * td)
            l = jnp.sum(l8[c], axis=0, keepdims=True)
            o_ref[:, cs] = (acc[c] / l * gate_ref[:, cs].astype(F32)).astype(BF16)


def _attn_online_kernel(q_ref, k_ref, v_ref, gate_ref, o_ref, kn_ref, vb_ref, *, blk):
    S = q_ref.shape[1]
    n_blk = S // blk

    for c in range(n_blk):
        cs = slice(c * blk, (c + 1) * blk)
        kn_ref[c] = k_ref[:, cs].astype(F32).T.astype(BF16)
        vb_ref[c] = v_ref[:, cs]

    kk = lax.broadcasted_iota(jnp.int32, (blk, blk), 0)
    qq = lax.broadcasted_iota(jnp.int32, (blk, blk), 1)
    causal = kk <= qq

    def step(q_t, k_blk, v_t, carry, mask):
        m, l, acc = carry
        s_t = jnp.dot(k_blk, q_t, preferred_element_type=F32)
        if mask:
            s_t = jnp.where(causal, s_t, -jnp.inf)
        m_new = jnp.maximum(m, jnp.max(s_t, axis=0, keepdims=True))
        alpha = jnp.exp(m - m_new)
        p = jnp.exp(s_t - m_new)
        l = alpha * l + jnp.sum(p, axis=0, keepdims=True)
        acc = alpha * acc + jnp.dot(v_t, p.astype(BF16), preferred_element_type=F32)
        return m_new, l, acc

    for qi in range(n_blk):
        qs = slice(qi * blk, (qi + 1) * blk)
        q_t = q_ref[:, qs]
        carry = (jnp.full((1, blk), -jnp.inf, F32), jnp.zeros((1, blk), F32),
                 jnp.zeros((HEAD_DIM, blk), F32))

        def body(kj, carry, q_t=q_t):
            return step(q_t, kn_ref[kj], vb_ref[kj], carry, mask=False)

        carry = lax.fori_loop(0, qi, body, carry)
        m, l, acc = step(q_t, kn_ref[qi], vb_ref[qi], carry, mask=True)
        o_ref[:, qs] = (acc / l * gate_ref[:, qs].astype(F32)).astype(BF16)


def _attention(qk_t, vg_t, n_heads, online_max):
    B, _, S = qk_t.shape
    D = n_heads * HEAD_DIM
    if online_max:
        blk = _tile(S, ATTN_ONLINE_BLOCK)
        body = functools.partial(_attn_online_kernel, blk=blk)
        scratch = [pltpu.VMEM((S // blk, blk, HEAD_ROWS), BF16),
                   pltpu.VMEM((S // blk, HEAD_DIM, blk), BF16)]
    else:
        tq = _tile(S, ATTN_QUERIES)
        tk = _tile(tq, ATTN_KEYS)
        td = _tile(tk, ATTN_DIAG)
        body = functools.partial(_attn_shifted_kernel, tq=tq, tk=tk, td=td)
        scratch = [pltpu.VMEM((S, HEAD_ROWS), BF16)]
    return pl.pallas_call(
        body,
        grid=(B, n_heads),
        in_specs=[
            pl.BlockSpec((None, HEAD_ROWS, S), lambda b, h: (b, h, 0)),
            pl.BlockSpec((None, HEAD_ROWS, S), lambda b, h: (b, h + n_heads, 0)),
            pl.BlockSpec((None, HEAD_DIM, S), lambda b, h: (b, h, 0)),
            pl.BlockSpec((None, HEAD_DIM, S), lambda b, h: (b, h + n_heads, 0)),
        ],
        out_specs=pl.BlockSpec((None, HEAD_DIM, S), lambda b, h: (b, h, 0)),
        out_shape=jax.ShapeDtypeStruct((B, D, S), BF16),
        scratch_shapes=scratch,
        compiler_params=_params(2),
        name="fox_attn_online" if online_max else "fox_attn",
    )(qk_t, qk_t, vg_t, vg_t)


def _out_proj_kernel(o_ref, w_ref, x_ref, y_ref):
    y_t = jnp.dot(w_ref[...], o_ref[...], preferred_element_type=F32)
    y_ref[...] = x_ref[...] + y_t.T


def _out_proj(o_t, w_out_t, layer, h):
    B, S, D = h.shape
    tm = _tile(S, OUT_TOKENS)
    return pl.pallas_call(
        _out_proj_kernel,
        grid=(B, S // tm),
        in_specs=[
            pl.BlockSpec((None, D, tm), lambda b, s: (b, 0, s)),
            pl.BlockSpec((None, D, D), lambda b, s: (layer, 0, 0)),
            pl.BlockSpec((None, tm, D), lambda b, s: (b, s, 0)),
        ],
        out_specs=pl.BlockSpec((None, tm, D), lambda b, s: (b, s, 0)),
        out_shape=jax.ShapeDtypeStruct((B, S, D), F32),
        compiler_params=_params(2),
        name="out_proj",
    )(o_t, w_out_t, h)


def _transpose_cast_kernel(w_ref, o_ref):
    o_ref[...] = w_ref[...].T.astype(BF16)


def _transpose_cast(w, n_cols):
    L, K, _ = w.shape
    tn = _tile(n_cols, WEIGHT_COLS)
    return pl.pallas_call(
        _transpose_cast_kernel,
        grid=(L, n_cols // tn),
        in_specs=[pl.BlockSpec((None, K, tn), lambda l, j: (l, 0, j))],
        out_specs=pl.BlockSpec((None, tn, K), lambda l, j: (l, j, 0)),
        out_shape=jax.ShapeDtypeStruct((L, n_cols, K), BF16),
        compiler_params=_params(2),
        name="weight_transpose",
    )(w)


def _fox_layer(h, g, w_in_t, b_f, gq, gk, w_out_t, layer):
    B, S, D = h.shape
    H = D // HEAD_DIM
    bound = (HEAD_DIM * ATTN_SCALE * BOUND_SLACK) * jnp.max(jnp.abs(gq)) * jnp.max(jnp.abs(gk))
    shifted = bound <= MAX_SHIFT
    shift = jnp.where(shifted, bound, 0.0).astype(F32)
    qk_t = _qk_proj(h, g, w_in_t, layer, b_f, gq, gk, shift)
    vg_t = _vg_proj(h, g, w_in_t, layer)
    o_t = lax.cond(shifted,
                   functools.partial(_attention, n_heads=H, online_max=False),
                   functools.partial(_attention, n_heads=H, online_max=True),
                   qk_t, vg_t)
    return _out_proj(o_t, w_out_t, layer, h)


def kernel(x, attn_norm_g, ffn_norm_g, final_norm_g, pool_w, pool_scale, fox_w_in, fox_b_f,
           fox_q_norm_g, fox_k_norm_g, fox_w_out, ffn_w_gate_up, ffn_w_down):
    B, S, D = x.shape
    depth = attn_norm_g.shape[0]
    fox_w_in_t = jnp.swapaxes(fox_w_in, 1, 2).astype(BF16)
    fox_w_out_t = _transpose_cast(fox_w_out, D)
    w_gu = ffn_w_gate_up.astype(BF16)
    w_d = ffn_w_down.astype(BF16)
    pool_w16 = pool_w.astype(BF16)
    h = x
    for i in range(depth):
        j = i // 2
        if i % 2 == 0:
            h = _pool_layer(h, attn_norm_g[i], pool_w16, j, pool_scale[j])
        else:
            h = _fox_layer(h, attn_norm_g[i], fox_w_in_t, fox_b_f[j], fox_q_norm_g[j],
                           fox_k_norm_g[j], fox_w_out_t, j)
        final_g = final_norm_g if i == depth - 1 else None
        h = _ffn(h.reshape(B * S, D), ffn_norm_g[i], w_gu, w_d, i, final_g).reshape(B, S, D)
    return h
```

```python
import functools

import jax
import jax.numpy as jnp
from jax import lax
from jax.experimental import pallas as pl
from jax.experimental.pallas import tpu as pltpu

HEAD_DIM = 64
POOL_WINDOWS = (2, 4, 8, 16)
RMS_EPS = 1e-6
ATTN_SCALE = HEAD_DIM ** -0.5

POOL_HIST = 128
HEAD_ROWS = 128
AUG_ROWS = 16
HEADS_PER_STEP = 16
MAX_SHIFT = 30.0
BOUND_SLACK = 1.0 + 2.0 ** -6
MASKED_LOGIT = -1e30

V7X_VMEM_LIMIT = 60000 * 1024

FFN_TOKENS = 1024
FFN_HIDDEN = 512
POOL_TOKENS = 1024
PROJ_TOKENS = 1024
PROJ_ROWS = 1024
ATTN_QUERIES = 2048
ATTN_KEYS = 2048
ATTN_DIAG = 512
ATTN_ONLINE_BLOCK = 512
OUT_TOKENS = 512
WEIGHT_COLS = 512

F32 = jnp.float32
BF16 = jnp.bfloat16


def _params(n_axes):
    return pltpu.CompilerParams(
        dimension_semantics=("arbitrary",) * n_axes,
        vmem_limit_bytes=V7X_VMEM_LIMIT,
    )


def _tile(n, want):
    t = min(n, want)
    assert n % t == 0, (n, want)
    return t


def _rms(x, g):
    ms = jnp.mean(x * x, axis=-1, keepdims=True)
    return x * lax.rsqrt(ms + RMS_EPS) * g


def _split3(x):
    x1 = x.astype(BF16).astype(F32)
    r = x - x1
    x2 = r.astype(BF16).astype(F32)
    x3 = (r - x2).astype(BF16).astype(F32)
    return x1, x2, x3


def _ffn_kernel(x_ref, g_ref, wg_ref, wu_ref, wd_ref, *rest, final_norm, cast_next):
    rest = list(rest)
    fg_ref = rest.pop(0) if final_norm else None
    if cast_next:
        ngu_ref, nd_ref, o_ref, ngu_out_ref, nd_out_ref, xn_ref = rest
        ngu_out_ref[...] = ngu_ref[...].astype(BF16)
        nd_out_ref[...] = nd_ref[...].astype(BF16)
    else:
        o_ref, xn_ref = rest
    j = pl.program_id(1)

    @pl.when(j == 0)
    def _():
        x = x_ref[...]
        xn_ref[...] = _rms(x, g_ref[...]).astype(BF16)
        o_ref[...] = x

    xn = xn_ref[...]
    gate = jnp.dot(xn, wg_ref[...], preferred_element_type=F32)
    up = jnp.dot(xn, wu_ref[...], preferred_element_type=F32)
    a = (gate * jax.nn.sigmoid(gate) * up).astype(BF16)
    o_ref[...] += jnp.dot(a, wd_ref[...], preferred_element_type=F32)

    if final_norm:
        @pl.when(j == pl.num_programs(1) - 1)
        def _():
            o_ref[...] = _rms(o_ref[...], fg_ref[...])


def _ffn(h, g, w_gu, w_d, layer, final_g=None, cast_next=None):
    M, D = h.shape
    F = w_d.shape[1]
    tm = _tile(M, FFN_TOKENS)
    tf = _tile(F, FFN_HIDDEN)
    nf = F // tf
    n_i = M // tm
    final_norm = final_g is not None
    in_specs = [
        pl.BlockSpec((tm, D), lambda i, j: (i, 0)),
        pl.BlockSpec((1, D), lambda i, j: (0, 0)),
        pl.BlockSpec((None, D, tf), lambda i, j: (layer, 0, j)),
        pl.BlockSpec((None, D, tf), lambda i, j: (layer, 0, j + nf)),
        pl.BlockSpec((None, tf, D), lambda i, j: (layer, j, 0)),
    ]
    args = [h, g.reshape(1, D), w_gu, w_gu, w_d]
    if final_norm:
        in_specs.append(pl.BlockSpec((1, D), lambda i, j: (0, 0)))
        args.append(final_g.reshape(1, D))
    out_specs = [pl.BlockSpec((tm, D), lambda i, j: (i, 0))]
    out_shape = [jax.ShapeDtypeStruct((M, D), F32)]
    if cast_next is not None:
        w_gu32, w_d32, nxt = cast_next
        gu_blk = (None, D // n_i, 2 * F // nf)
        d_blk = (None, F // nf, D // n_i)
        assert D % n_i == 0 and gu_blk[1] % 16 == 0 and gu_blk[2] % 128 == 0 and d_blk[2] % 128 == 0
        in_specs += [pl.BlockSpec(gu_blk, lambda i, j: (nxt, i, j)),
                     pl.BlockSpec(d_blk, lambda i, j: (nxt, j, i))]
        args += [w_gu32, w_d32]
        out_specs += [pl.BlockSpec(gu_blk, lambda i, j: (0, i, j)),
                      pl.BlockSpec(d_blk, lambda i, j: (0, j, i))]
        out_shape += [jax.ShapeDtypeStruct((1, D, 2 * F), BF16),
                      jax.ShapeDtypeStruct((1, F, D), BF16)]
    outs = pl.pallas_call(
        functools.partial(_ffn_kernel, final_norm=final_norm, cast_next=cast_next is not None),
        grid=(n_i, nf),
        in_specs=in_specs,
        out_specs=out_specs,
        out_shape=out_shape,
        scratch_shapes=[pltpu.VMEM((tm, D), BF16)],
        compiler_params=_params(2),
        name="ffn",
    )(*args)
    return outs if cast_next is not None else outs[0]


def _pool_kernel(x_ref, g_ref, band_ref, cnt_ref, w_ref, sc_ref, o_ref, hi_ref, lo_ref, *,
                 n_groups):
    si = pl.program_id(1)
    tm, D = x_ref.shape
    C = D // n_groups

    @pl.when(si == 0)
    def _():
        hi_ref[:POOL_HIST, :] = jnp.zeros((POOL_HIST, D), BF16)
        lo_ref[:POOL_HIST, :] = jnp.zeros((POOL_HIST, D), BF16)

    x = x_ref[...]
    hn = _rms(x, g_ref[...])
    hi = hn.astype(BF16)
    hi_ref[POOL_HIST:, :] = hi
    lo_ref[POOL_HIST:, :] = (hn - hi.astype(F32)).astype(BF16)

    for gi in range(n_groups):
        cols = slice(gi * C, (gi + 1) * C)
        band = band_ref[gi]
        slabs = []
        for r0 in range(0, tm, POOL_HIST):
            rows = slice(r0, r0 + 2 * POOL_HIST)
            slabs.append(jnp.dot(band, hi_ref[rows, cols], preferred_element_type=F32)
                         + jnp.dot(band, lo_ref[rows, cols], preferred_element_type=F32))
        win = jnp.concatenate(slabs, axis=0)
        feat = win / cnt_ref[:, gi:gi + 1] - hn[:, cols]
        y = jnp.dot(feat.astype(BF16), w_ref[gi], preferred_element_type=F32)
        o_ref[:, cols] = x[:, cols] + y * sc_ref[:, cols]

    hi_ref[:POOL_HIST, :] = hi_ref[tm:, :]
    lo_ref[:POOL_HIST, :] = lo_ref[tm:, :]


def _pool_layer(h, g, w_group, layer, scale):
    B, S, D = h.shape
    G = len(POOL_WINDOWS)
    C = D // G
    tm = _tile(S, POOL_TOKENS)
    assert tm % POOL_HIST == 0 and POOL_HIST >= max(POOL_WINDOWS)
    r = jnp.arange(POOL_HIST)[:, None] + POOL_HIST
    c = jnp.arange(2 * POOL_HIST)[None, :]
    band = jnp.stack([((c <= r) & (c > r - w)) for w in POOL_WINDOWS]).astype(BF16)
    n_valid = jnp.arange(1, S + 1, dtype=F32)[:, None]
    cnt = jnp.minimum(n_valid, jnp.asarray(POOL_WINDOWS, F32)[None, :])
    return pl.pallas_call(
        functools.partial(_pool_kernel, n_groups=G),
        grid=(B, S // tm),
        in_specs=[
            pl.BlockSpec((None, tm, D), lambda b, s: (b, s, 0)),
            pl.BlockSpec((1, D), lambda b, s: (0, 0)),
            pl.BlockSpec((G, POOL_HIST, 2 * POOL_HIST), lambda b, s: (0, 0, 0)),
            pl.BlockSpec((tm, G), lambda b, s: (s, 0)),
            pl.BlockSpec((None, G, C, C), lambda b, s: (layer, 0, 0, 0)),
            pl.BlockSpec((1, D), lambda b, s: (0, 0)),
        ],
        out_specs=pl.BlockSpec((None, tm, D), lambda b, s: (b, s, 0)),
        out_shape=jax.ShapeDtypeStruct((B, S, D), F32),
        scratch_shapes=[
            pltpu.VMEM((POOL_HIST + tm, D), BF16), pltpu.VMEM((POOL_HIST + tm, D), BF16),
        ],
        compiler_params=_params(2),
        name="pool",
    )(h, g.reshape(1, D), band, cnt, w_group, scale.reshape(1, D))


def _qk_proj_kernel(x_ref, g_ref, w_ref, wf_ref, bf_ref, tri_ref, gq_ref, gk_ref, shift_ref,
                    o_ref, xn_ref, c_ref, carry_ref, *, n_q_steps):
    si = pl.program_id(1)
    j = pl.program_id(2)
    tm = x_ref.shape[0]

    @pl.when(j == 0)
    def _():
        xn = _rms(x_ref[...], g_ref[...]).astype(BF16)
        xn_ref[...] = xn
        z = lax.dot_general(wf_ref[...], xn, (((1,), (1,)), ((), ())),
                            preferred_element_type=F32) + bf_ref[...]
        logf = jnp.minimum(z, 0.0) - jnp.log1p(jnp.exp(-jnp.abs(z)))
        f1, f2, f3 = _split3(logf)
        tri = tri_ref[...]
        csum = (jnp.dot(f1.astype(BF16), tri, preferred_element_type=F32)
                + jnp.dot(f2.astype(BF16), tri, preferred_element_type=F32)
                + jnp.dot(f3.astype(BF16), tri, preferred_element_type=F32))
        prev = jnp.where(si == 0, 0.0, carry_ref[...])
        csum = csum + prev
        carry_ref[...] = csum[:, tm - 1:tm]
        for side, val in ((0, csum - shift_ref[...]), (1, -csum)):
            for part, term in enumerate(_split3(val)):
                c_ref[side, part] = term

    y = lax.dot_general(w_ref[...], xn_ref[...], (((1,), (1,)), ((), ())),
                        preferred_element_type=F32)
    heads = w_ref.shape[0] // HEAD_DIM
    is_q = j < n_q_steps
    gain = jnp.where(is_q, gq_ref[...] * ATTN_SCALE, gk_ref[...])
    side = jnp.where(is_q, 0, 1)
    c_row0 = jnp.where(is_q, 0, 3)
    head0 = jnp.where(is_q, j, j - n_q_steps) * heads
    row = lax.broadcasted_iota(jnp.int32, (AUG_ROWS, tm), 0)
    for hh in range(heads):
        yh = y[hh * HEAD_DIM:(hh + 1) * HEAD_DIM]
        ms = jnp.mean(yh * yh, axis=0, keepdims=True)
        yn = yh * lax.rsqrt(ms + RMS_EPS) * gain
        hd = head0 + hh
        c1 = c_ref[side, 0, pl.ds(hd, 1), :]
        c2 = c_ref[side, 1, pl.ds(hd, 1), :]
        c3 = c_ref[side, 2, pl.ds(hd, 1), :]
        sel = row - c_row0
        in_c = (sel >= 0) & (sel < 3)
        in_one = (row >= 3 - c_row0) & (row < 6 - c_row0)
        cval = jnp.where(sel == 0, c1, jnp.where(sel == 1, c2, c3))
        aug = jnp.where(in_c, cval, jnp.where(in_one, 1.0, 0.0))
        base = hh * HEAD_ROWS
        o_ref[base:base + HEAD_DIM, :] = yn.astype(BF16)
        o_ref[base + HEAD_DIM:base + HEAD_DIM + AUG_ROWS, :] = aug.astype(BF16)
        o_ref[base + HEAD_DIM + AUG_ROWS:base + HEAD_ROWS, :] = jnp.zeros(
            (HEAD_ROWS - HEAD_DIM - AUG_ROWS, tm), BF16)


def _qk_proj(h, g, w_in_t, layer, b_f, gq, gk, shift):
    B, S, D = h.shape
    H = D // HEAD_DIM
    f_block = 4 * D // H
    tm = _tile(S, PROJ_TOKENS)
    heads = _tile(H, HEADS_PER_STEP)
    rows = heads * HEAD_DIM
    n_q_steps = D // rows
    tri = (jnp.arange(tm)[:, None] <= jnp.arange(tm)[None, :]).astype(BF16)
    return pl.pallas_call(
        functools.partial(_qk_proj_kernel, n_q_steps=n_q_steps),
        grid=(B, S // tm, 2 * n_q_steps),
        in_specs=[
            pl.BlockSpec((None, tm, D), lambda b, s, j: (b, s, 0)),
            pl.BlockSpec((1, D), lambda b, s, j: (0, 0)),
            pl.BlockSpec((None, rows, D), lambda b, s, j: (layer, j, 0)),
            pl.BlockSpec((None, H, D), lambda b, s, j: (layer, f_block, 0)),
            pl.BlockSpec((H, 1), lambda b, s, j: (0, 0)),
            pl.BlockSpec((tm, tm), lambda b, s, j: (0, 0)),
            pl.BlockSpec((HEAD_DIM, 1), lambda b, s, j: (0, 0)),
            pl.BlockSpec((HEAD_DIM, 1), lambda b, s, j: (0, 0)),
            pl.BlockSpec((1, 1), lambda b, s, j: (0, 0)),
        ],
        out_specs=pl.BlockSpec((None, heads * HEAD_ROWS, tm), lambda b, s, j: (b, j, s)),
        out_shape=jax.ShapeDtypeStruct((B, 2 * H * HEAD_ROWS, S), BF16),
        scratch_shapes=[
            pltpu.VMEM((tm, D), BF16),
            pltpu.VMEM((2, 3, H, tm), F32),
            pltpu.VMEM((H, 1), F32),
        ],
        compiler_params=_params(3),
        name="qk_proj",
    )(h, g.reshape(1, D), w_in_t, w_in_t, b_f.reshape(H, 1), tri,
      gq.reshape(HEAD_DIM, 1), gk.reshape(HEAD_DIM, 1), shift.reshape(1, 1))


def _vg_proj_kernel(x_ref, g_ref, w_ref, o_ref, xn_ref, *, n_v_steps):
    j = pl.program_id(2)

    @pl.when(j == 0)
    def _():
        xn_ref[...] = _rms(x_ref[...], g_ref[...]).astype(BF16)

    y = lax.dot_general(w_ref[...], xn_ref[...], (((1,), (1,)), ((), ())),
                        preferred_element_type=F32)

    @pl.when(j < n_v_steps)
    def _():
        o_ref[...] = y.astype(BF16)

    @pl.when(j >= n_v_steps)
    def _():
        o_ref[...] = jax.nn.sigmoid(y).astype(BF16)


def _vg_proj(h, g, w_in_t, layer):
    B, S, D = h.shape
    tm = _tile(S, PROJ_TOKENS)
    tn = _tile(D, PROJ_ROWS)
    n_v_steps = D // tn
    row0 = 2 * n_v_steps
    return pl.pallas_call(
        functools.partial(_vg_proj_kernel, n_v_steps=n_v_steps),
        grid=(B, S // tm, 2 * n_v_steps),
        in_specs=[
            pl.BlockSpec((None, tm, D), lambda b, s, j: (b, s, 0)),
            pl.BlockSpec((1, D), lambda b, s, j: (0, 0)),
            pl.BlockSpec((None, tn, D), lambda b, s, j: (layer, row0 + j, 0)),
        ],
        out_specs=pl.BlockSpec((None, tn, tm), lambda b, s, j: (b, j, s)),
        out_shape=jax.ShapeDtypeStruct((B, 2 * D, S), BF16),
        scratch_shapes=[pltpu.VMEM((tm, D), BF16)],
        compiler_params=_params(3),
        name="vg_proj",
    )(h, g.reshape(1, D), w_in_t)


def _attn_shifted_kernel(q_ref, k_ref, v_ref, gate_ref, o_ref, kn_ref, *, tq, tk, td):
    S = q_ref.shape[1]
    n_chunks = tq // td
    for c in range(S // td):
        cs = slice(c * td, (c + 1) * td)
        kn_ref[cs, :] = k_ref[:, cs].astype(F32).T.astype(BF16)

    kk = lax.broadcasted_iota(jnp.int32, (td, td), 0)
    qq = lax.broadcasted_iota(jnp.int32, (td, td), 1)
    causal = kk <= qq

    def probs(k_rows, q_cols, v_cols, mask):
        s_t = jnp.dot(kn_ref[k_rows, :], q_cols, preferred_element_type=F32)
        if mask:
            s_t = jnp.where(causal, s_t, MASKED_LOGIT)
        p = jnp.exp(s_t)
        pv = jnp.dot(v_ref[:, v_cols], p.astype(BF16), preferred_element_type=F32)
        return p, pv

    def row_sums8(p):
        return jnp.sum(p.reshape(p.shape[0] // 8, 8, p.shape[1]), axis=0)

    for qi in range(S // tq):
        q0 = qi * tq
        q_t = q_ref[:, q0:q0 + tq]
        acc = [jnp.zeros((HEAD_DIM, td), F32) for _ in range(n_chunks)]
        l8 = [jnp.zeros((8, td), F32) for _ in range(n_chunks)]
        for kj in range(q0 // tk):
            ks = slice(kj * tk, (kj + 1) * tk)
            p, pv = probs(ks, q_t, ks, mask=False)
            for c in range(n_chunks):
                cc = slice(c * td, (c + 1) * td)
                acc[c] = acc[c] + pv[:, cc]
                l8[c] = l8[c] + row_sums8(p[:, cc])
        for d in range(n_chunks):
            ks = slice(q0 + d * td, q0 + (d + 1) * td)
            p, pv = probs(ks, q_t[:, d * td:(d + 1) * td], ks, mask=True)
            acc[d] = acc[d] + pv
            l8[d] = l8[d] + row_sums8(p)
            if d + 1 < n_chunks:
                p, pv = probs(ks, q_t[:, (d + 1) * td:], ks, mask=False)
                for c in range(d + 1, n_chunks):
                    cc = slice((c - d - 1) * td, (c - d) * td)
                    acc[c] = acc[c] + pv[:, cc]
                    l8[c] = l8[c] + row_sums8(p[:, cc])
        for c in range(n_chunks):
            cs = slice(q0 + c * td, q0 + (c + 1) * td)
            l = jnp.sum(l8[c], axis=0, keepdims=True)
            o_ref[:, cs] = (acc[c] / l * gate_ref[:, cs].astype(F32)).astype(BF16)


def _attn_online_kernel(q_ref, k_ref, v_ref, gate_ref, o_ref, kn_ref, vb_ref, *, blk):
    S = q_ref.shape[1]
    n_blk = S // blk

    for c in range(n_blk):
        cs = slice(c * blk, (c + 1) * blk)
        kn_ref[c] = k_ref[:, cs].astype(F32).T.astype(BF16)
        vb_ref[c] = v_ref[:, cs]

    kk = lax.broadcasted_iota(jnp.int32, (blk, blk), 0)
    qq = lax.broadcasted_iota(jnp.int32, (blk, blk), 1)
    causal = kk <= qq

    def step(q_t, k_blk, v_t, carry, mask):
        m, l, acc = carry
        s_t = jnp.dot(k_blk, q_t, preferred_element_type=F32)
        if mask:
            s_t = jnp.where(causal, s_t, -jnp.inf)
        m_new = jnp.maximum(m, jnp.max(s_t, axis=0, keepdims=True))
        alpha = jnp.exp(m - m_new)
        p = jnp.exp(s_t - m_new)
        l = alpha * l + jnp.sum(p, axis=0, keepdims=True)
        acc = alpha * acc + jnp.dot(v_t, p.astype(BF16), preferred_element_type=F32)
        return m_new, l, acc

    for qi in range(n_blk):
        qs = slice(qi * blk, (qi + 1) * blk)
        q_t = q_ref[:, qs]
        carry = (jnp.full((1, blk), -jnp.inf, F32), jnp.zeros((1, blk), F32),
                 jnp.zeros((HEAD_DIM, blk), F32))

        def body(kj, carry, q_t=q_t):
            return step(q_t, kn_ref[kj], vb_ref[kj], carry, mask=False)

        carry = lax.fori_loop(0, qi, body, carry)
        m, l, acc = step(q_t, kn_ref[qi], vb_ref[qi], carry, mask=True)
        o_ref[:, qs] = (acc / l * gate_ref[:, qs].astype(F32)).astype(BF16)


def _attention(qk_t, vg_t, n_heads, online_max):
    B, _, S = qk_t.shape
    D = n_heads * HEAD_DIM
    if online_max:
        blk = _tile(S, ATTN_ONLINE_BLOCK)
        body = functools.partial(_attn_online_kernel, blk=blk)
        scratch = [pltpu.VMEM((S // blk, blk, HEAD_ROWS), BF16),
                   pltpu.VMEM((S // blk, HEAD_DIM, blk), BF16)]
    else:
        tq = _tile(S, ATTN_QUERIES)
        tk = _tile(tq, ATTN_KEYS)
        td = _tile(tk, ATTN_DIAG)
        body = functools.partial(_attn_shifted_kernel, tq=tq, tk=tk, td=td)
        scratch = [pltpu.VMEM((S, HEAD_ROWS), BF16)]
    return pl.pallas_call(
        body,
        grid=(B, n_heads),
        in_specs=[
            pl.BlockSpec((None, HEAD_ROWS, S), lambda b, h: (b, h, 0)),
            pl.BlockSpec((None, HEAD_ROWS, S), lambda b, h: (b, h + n_heads, 0)),
            pl.BlockSpec((None, HEAD_DIM, S), lambda b, h: (b, h, 0)),
            pl.BlockSpec((None, HEAD_DIM, S), lambda b, h: (b, h + n_heads, 0)),
        ],
        out_specs=pl.BlockSpec((None, HEAD_DIM, S), lambda b, h: (b, h, 0)),
        out_shape=jax.ShapeDtypeStruct((B, D, S), BF16),
        scratch_shapes=scratch,
        compiler_params=_params(2),
        name="fox_attn_online" if online_max else "fox_attn",
    )(qk_t, qk_t, vg_t, vg_t)


def _out_proj_kernel(o_ref, w_ref, x_ref, y_ref):
    y_t = jnp.dot(w_ref[...], o_ref[...], preferred_element_type=F32)
    y_ref[...] = x_ref[...] + y_t.T


def _out_proj(o_t, w_out_t, layer, h):
    B, S, D = h.shape
    tm = _tile(S, OUT_TOKENS)
    return pl.pallas_call(
        _out_proj_kernel,
        grid=(B, S // tm),
        in_specs=[
            pl.BlockSpec((None, D, tm), lambda b, s: (b, 0, s)),
            pl.BlockSpec((None, D, D), lambda b, s: (layer, 0, 0)),
            pl.BlockSpec((None, tm, D), lambda b, s: (b, s, 0)),
        ],
        out_specs=pl.BlockSpec((None, tm, D), lambda b, s: (b, s, 0)),
        out_shape=jax.ShapeDtypeStruct((B, S, D), F32),
        compiler_params=_params(2),
        name="out_proj",
    )(o_t, w_out_t, h)


def _transpose_cast_kernel(w_ref, o_ref):
    o_ref[...] = w_ref[...].T.astype(BF16)


def _transpose_cast(w, n_cols):
    L, K, _ = w.shape
    tn = _tile(n_cols, WEIGHT_COLS)
    return pl.pallas_call(
        _transpose_cast_kernel,
        grid=(L, n_cols // tn),
        in_specs=[pl.BlockSpec((None, K, tn), lambda l, j: (l, 0, j))],
        out_specs=pl.BlockSpec((None, tn, K), lambda l, j: (l, j, 0)),
        out_shape=jax.ShapeDtypeStruct((L, n_cols, K), BF16),
        compiler_params=_params(2),
        name="weight_transpose",
    )(w)


def _fox_layer(h, g, w_in_t, b_f, gq, gk, w_out_t, layer):
    B, S, D = h.shape
    H = D // HEAD_DIM
    bound = (HEAD_DIM * ATTN_SCALE * BOUND_SLACK) * jnp.max(jnp.abs(gq)) * jnp.max(jnp.abs(gk))
    shifted = bound <= MAX_SHIFT
    shift = jnp.where(shifted, bound, 0.0).astype(F32)
    qk_t = _qk_proj(h, g, w_in_t, layer, b_f, gq, gk, shift)
    vg_t = _vg_proj(h, g, w_in_t, layer)
    o_t = lax.cond(shifted,
                   functools.partial(_attention, n_heads=H, online_max=False),
                   functools.partial(_attention, n_heads=H, online_max=True),
                   qk_t, vg_t)
    return _out_proj(o_t, w_out_t, layer, h)


def kernel(x, attn_norm_g, ffn_norm_g, final_norm_g, pool_w, pool_scale, fox_w_in, fox_b_f,
           fox_q_norm_g, fox_k_norm_g, fox_w_out, ffn_w_gate_up, ffn_w_down):
    B, S, D = x.shape
    depth = attn_norm_g.shape[0]
    fox_w_in_t = jnp.swapaxes(fox_w_in, 1, 2).astype(BF16)
    fox_w_out_t = _transpose_cast(fox_w_out, D)
    w_gu = ffn_w_gate_up[:1].astype(BF16)
    w_d = ffn_w_down[:1].astype(BF16)
    pool_w16 = pool_w.astype(BF16)
    h = x
    for i in range(depth):
        j = i // 2
        if i % 2 == 0:
            h = _pool_layer(h, attn_norm_g[i], pool_w16, j, pool_scale[j])
        else:
            h = _fox_layer(h, attn_norm_g[i], fox_w_in_t, fox_b_f[j], fox_q_norm_g[j],
                           fox_k_norm_g[j], fox_w_out_t, j)
        if i == depth - 1:
            h = _ffn(h.reshape(B * S, D), ffn_norm_g[i], w_gu, w_d, 0, final_g=final_norm_g)
        else:
            h, w_gu, w_d = _ffn(h.reshape(B * S, D), ffn_norm_g[i], w_gu, w_d, 0,
                                cast_next=(ffn_w_gate_up, ffn_w_down, i + 1))
        h = h.reshape(B, S, D)
    return h
```

```python
import functools

import jax
import jax.numpy as jnp
from jax import lax
from jax.experimental import pallas as pl
from jax.experimental.pallas import tpu as pltpu

HEAD_DIM = 64
POOL_WINDOWS = (2, 4, 8, 16)
RMS_EPS = 1e-6
ATTN_SCALE = HEAD_DIM ** -0.5

POOL_HIST = 128
HEAD_ROWS = 128
AUG_ROWS = 16
HEADS_PER_STEP = 16
MAX_SHIFT = 30.0
BOUND_SLACK = 1.0 + 2.0 ** -6
MASKED_LOGIT = -1e30

V7X_VMEM_LIMIT = 60000 * 1024

FFN_TOKENS = 1024
FFN_HIDDEN = 512
FFN_HIDDEN_CHUNK = 256
POOL_TOKENS = 1024
PROJ_TOKENS = 1024
PROJ_ROWS = 1024
ATTN_QUERIES = 2048
ATTN_KEYS = 2048
ATTN_DIAG = 512
ATTN_ONLINE_BLOCK = 512
OUT_TOKENS = 512
WEIGHT_COLS = 512

F32 = jnp.float32
BF16 = jnp.bfloat16


def _params(n_axes):
    return pltpu.CompilerParams(
        dimension_semantics=("arbitrary",) * n_axes,
        vmem_limit_bytes=V7X_VMEM_LIMIT,
    )


def _tile(n, want):
    t = min(n, want)
    assert n % t == 0, (n, want)
    return t


def _rms(x, g):
    ms = jnp.mean(x * x, axis=-1, keepdims=True)
    return x * lax.rsqrt(ms + RMS_EPS) * g


def _split3(x):
    x1 = x.astype(BF16).astype(F32)
    r = x - x1
    x2 = r.astype(BF16).astype(F32)
    x3 = (r - x2).astype(BF16).astype(F32)
    return x1, x2, x3


def _ffn_kernel(x_ref, g_ref, wg_ref, wu_ref, wd_ref, *rest, final_norm, cast_next):
    rest = list(rest)
    fg_ref = rest.pop(0) if final_norm else None
    if cast_next:
        ngu_ref, nd_ref, o_ref, ngu_out_ref, nd_out_ref, xn_ref = rest
        ngu_out_ref[...] = ngu_ref[...].astype(BF16)
        nd_out_ref[...] = nd_ref[...].astype(BF16)
    else:
        o_ref, xn_ref = rest
    j = pl.program_id(1)

    @pl.when(j == 0)
    def _():
        x = x_ref[...]
        xn_ref[...] = _rms(x, g_ref[...]).astype(BF16)
        o_ref[...] = x

    xn = xn_ref[...]
    tf = wg_ref.shape[1]
    contrib = None
    for c0 in range(0, tf, FFN_HIDDEN_CHUNK):
        cc = slice(c0, c0 + FFN_HIDDEN_CHUNK)
        gate = jnp.dot(xn, wg_ref[:, cc], preferred_element_type=F32)
        up = jnp.dot(xn, wu_ref[:, cc], preferred_element_type=F32)
        a = (gate * jax.nn.sigmoid(gate) * up).astype(BF16)
        part = jnp.dot(a, wd_ref[cc, :], preferred_element_type=F32)
        contrib = part if contrib is None else contrib + part
    o_ref[...] += contrib

    if final_norm:
        @pl.when(j == pl.num_programs(1) - 1)
        def _():
            o_ref[...] = _rms(o_ref[...], fg_ref[...])


def _ffn(h, g, w_gu, w_d, layer, final_g=None, cast_next=None):
    M, D = h.shape
    F = w_d.shape[1]
    tm = _tile(M, FFN_TOKENS)
    tf = _tile(F, FFN_HIDDEN)
    nf = F // tf
    n_i = M // tm
    final_norm = final_g is not None
    in_specs = [
        pl.BlockSpec((tm, D), lambda i, j: (i, 0)),
        pl.BlockSpec((1, D), lambda i, j: (0, 0)),
        pl.BlockSpec((None, D, tf), lambda i, j: (layer, 0, j)),
        pl.BlockSpec((None, D, tf), lambda i, j: (layer, 0, j + nf)),
        pl.BlockSpec((None, tf, D), lambda i, j: (layer, j, 0)),
    ]
    args = [h, g.reshape(1, D), w_gu, w_gu, w_d]
    if final_norm:
        in_specs.append(pl.BlockSpec((1, D), lambda i, j: (0, 0)))
        args.append(final_g.reshape(1, D))
    out_specs = [pl.BlockSpec((tm, D), lambda i, j: (i, 0))]
    out_shape = [jax.ShapeDtypeStruct((M, D), F32)]
    if cast_next is not None:
        w_gu32, w_d32, nxt = cast_next
        gu_blk = (None, D // n_i, 2 * F // nf)
        d_blk = (None, F // nf, D // n_i)
        assert D % n_i == 0 and gu_blk[1] % 16 == 0 and gu_blk[2] % 128 == 0 and d_blk[2] % 128 == 0
        in_specs += [pl.BlockSpec(gu_blk, lambda i, j: (nxt, i, j)),
                     pl.BlockSpec(d_blk, lambda i, j: (nxt, j, i))]
        args += [w_gu32, w_d32]
        out_specs += [pl.BlockSpec(gu_blk, lambda i, j: (0, i, j)),
                      pl.BlockSpec(d_blk, lambda i, j: (0, j, i))]
        out_shape += [jax.ShapeDtypeStruct((1, D, 2 * F), BF16),
                      jax.ShapeDtypeStruct((1, F, D), BF16)]
    outs = pl.pallas_call(
        functools.partial(_ffn_kernel, final_norm=final_norm, cast_next=cast_next is not None),
        grid=(n_i, nf),
        in_specs=in_specs,
        out_specs=out_specs,
        out_shape=out_shape,
        scratch_shapes=[pltpu.VMEM((tm, D), BF16)],
        compiler_params=_params(2),
        name="ffn",
    )(*args)
    return outs if cast_next is not None else outs[0]


def _pool_kernel(x_ref, g_ref, band_ref, cnt_ref, w_ref, sc_ref, o_ref, hi_ref, lo_ref, *,
                 n_groups):
    si = pl.program_id(1)
    tm, D = x_ref.shape
    C = D // n_groups

    @pl.when(si == 0)
    def _():
        hi_ref[:POOL_HIST, :] = jnp.zeros((POOL_HIST, D), BF16)
        lo_ref[:POOL_HIST, :] = jnp.zeros((POOL_HIST, D), BF16)

    x = x_ref[...]
    hn = _rms(x, g_ref[...])
    hi = hn.astype(BF16)
    hi_ref[POOL_HIST:, :] = hi
    lo_ref[POOL_HIST:, :] = (hn - hi.astype(F32)).astype(BF16)

    for gi in range(n_groups):
        cols = slice(gi * C, (gi + 1) * C)
        band = band_ref[gi]
        slabs = []
        for r0 in range(0, tm, POOL_HIST):
            rows = slice(r0, r0 + 2 * POOL_HIST)
            slabs.append(jnp.dot(band, hi_ref[rows, cols], preferred_element_type=F32)
                         + jnp.dot(band, lo_ref[rows, cols], preferred_element_type=F32))
        win = jnp.concatenate(slabs, axis=0)
        feat = win / cnt_ref[:, gi:gi + 1] - hn[:, cols]
        y = jnp.dot(feat.astype(BF16), w_ref[gi], preferred_element_type=F32)
        o_ref[:, cols] = x[:, cols] + y * sc_ref[:, cols]

    hi_ref[:POOL_HIST, :] = hi_ref[tm:, :]
    lo_ref[:POOL_HIST, :] = lo_ref[tm:, :]


def _pool_layer(h, g, w_group, layer, scale):
    B, S, D = h.shape
    G = len(POOL_WINDOWS)
    C = D // G
    tm = _tile(S, POOL_TOKENS)
    assert tm % POOL_HIST == 0 and POOL_HIST >= max(POOL_WINDOWS)
    r = jnp.arange(POOL_HIST)[:, None] + POOL_HIST
    c = jnp.arange(2 * POOL_HIST)[None, :]
    band = jnp.stack([((c <= r) & (c > r - w)) for w in POOL_WINDOWS]).astype(BF16)
    n_valid = jnp.arange(1, S + 1, dtype=F32)[:, None]
    cnt = jnp.minimum(n_valid, jnp.asarray(POOL_WINDOWS, F32)[None, :])
    return pl.pallas_call(
        functools.partial(_pool_kernel, n_groups=G),
        grid=(B, S // tm),
        in_specs=[
            pl.BlockSpec((None, tm, D), lambda b, s: (b, s, 0)),
            pl.BlockSpec((1, D), lambda b, s: (0, 0)),
            pl.BlockSpec((G, POOL_HIST, 2 * POOL_HIST), lambda b, s: (0, 0, 0)),
            pl.BlockSpec((tm, G), lambda b, s: (s, 0)),
            pl.BlockSpec((None, G, C, C), lambda b, s: (layer, 0, 0, 0)),
            pl.BlockSpec((1, D), lambda b, s: (0, 0)),
        ],
        out_specs=pl.BlockSpec((None, tm, D), lambda b, s: (b, s, 0)),
        out_shape=jax.ShapeDtypeStruct((B, S, D), F32),
        scratch_shapes=[
            pltpu.VMEM((POOL_HIST + tm, D), BF16), pltpu.VMEM((POOL_HIST + tm, D), BF16),
        ],
        compiler_params=_params(2),
        name="pool",
    )(h, g.reshape(1, D), band, cnt, w_group, scale.reshape(1, D))


def _qk_proj_kernel(x_ref, g_ref, w_ref, wf_ref, bf_ref, tri_ref, gq_ref, gk_ref, shift_ref,
                    o_ref, xn_ref, c_ref, carry_ref, *, n_q_steps):
    si = pl.program_id(1)
    j = pl.program_id(2)
    tm = x_ref.shape[0]

    @pl.when(j == 0)
    def _():
        xn = _rms(x_ref[...], g_ref[...]).astype(BF16)
        xn_ref[...] = xn
        z = lax.dot_general(wf_ref[...], xn, (((1,), (1,)), ((), ())),
                            preferred_element_type=F32) + bf_ref[...]
        logf = jnp.minimum(z, 0.0) - jnp.log1p(jnp.exp(-jnp.abs(z)))
        f1, f2, f3 = _split3(logf)
        tri = tri_ref[...]
        csum = (jnp.dot(f1.astype(BF16), tri, preferred_element_type=F32)
                + jnp.dot(f2.astype(BF16), tri, preferred_element_type=F32)
                + jnp.dot(f3.astype(BF16), tri, preferred_element_type=F32))
        prev = jnp.where(si == 0, 0.0, carry_ref[...])
        csum = csum + prev
        carry_ref[...] = csum[:, tm - 1:tm]
        for side, val in ((0, csum - shift_ref[...]), (1, -csum)):
            for part, term in enumerate(_split3(val)):
                c_ref[side, part] = term

    y = lax.dot_general(w_ref[...], xn_ref[...], (((1,), (1,)), ((), ())),
                        preferred_element_type=F32)
    heads = w_ref.shape[0] // HEAD_DIM
    is_q = j < n_q_steps
    gain = jnp.where(is_q, gq_ref[...] * ATTN_SCALE, gk_ref[...])
    side = jnp.where(is_q, 0, 1)
    c_row0 = jnp.where(is_q, 0, 3)
    head0 = jnp.where(is_q, j, j - n_q_steps) * heads
    row = lax.broadcasted_iota(jnp.int32, (AUG_ROWS, tm), 0)
    for hh in range(heads):
        yh = y[hh * HEAD_DIM:(hh + 1) * HEAD_DIM]
        ms = jnp.mean(yh * yh, axis=0, keepdims=True)
        yn = yh * lax.rsqrt(ms + RMS_EPS) * gain
        hd = head0 + hh
        c1 = c_ref[side, 0, pl.ds(hd, 1), :]
        c2 = c_ref[side, 1, pl.ds(hd, 1), :]
        c3 = c_ref[side, 2, pl.ds(hd, 1), :]
        sel = row - c_row0
        in_c = (sel >= 0) & (sel < 3)
        in_one = (row >= 3 - c_row0) & (row < 6 - c_row0)
        cval = jnp.where(sel == 0, c1, jnp.where(sel == 1, c2, c3))
        aug = jnp.where(in_c, cval, jnp.where(in_one, 1.0, 0.0))
        base = hh * HEAD_ROWS
        o_ref[base:base + HEAD_DIM, :] = yn.astype(BF16)
        o_ref[base + HEAD_DIM:base + HEAD_DIM + AUG_ROWS, :] = aug.astype(BF16)
        o_ref[base + HEAD_DIM + AUG_ROWS:base + HEAD_ROWS, :] = jnp.zeros(
            (HEAD_ROWS - HEAD_DIM - AUG_ROWS, tm), BF16)


def _qk_proj(h, g, w_in_t, layer, b_f, gq, gk, shift):
    B, S, D = h.shape
    H = D // HEAD_DIM
    f_block = 4 * D // H
    tm = _tile(S, PROJ_TOKENS)
    heads = _tile(H, HEADS_PER_STEP)
    rows = heads * HEAD_DIM
    n_q_steps = D // rows
    tri = (jnp.arange(tm)[:, None] <= jnp.arange(tm)[None, :]).astype(BF16)
    return pl.pallas_call(
        functools.partial(_qk_proj_kernel, n_q_steps=n_q_steps),
        grid=(B, S // tm, 2 * n_q_steps),
        in_specs=[
            pl.BlockSpec((None, tm, D), lambda b, s, j: (b, s, 0)),
            pl.BlockSpec((1, D), lambda b, s, j: (0, 0)),
            pl.BlockSpec((None, rows, D), lambda b, s, j: (layer, j, 0)),
            pl.BlockSpec((None, H, D), lambda b, s, j: (layer, f_block, 0)),
            pl.BlockSpec((H, 1), lambda b, s, j: (0, 0)),
            pl.BlockSpec((tm, tm), lambda b, s, j: (0, 0)),
            pl.BlockSpec((HEAD_DIM, 1), lambda b, s, j: (0, 0)),
            pl.BlockSpec((HEAD_DIM, 1), lambda b, s, j: (0, 0)),
            pl.BlockSpec((1, 1), lambda b, s, j: (0, 0)),
        ],
        out_specs=pl.BlockSpec((None, heads * HEAD_ROWS, tm), lambda b, s, j: (b, j, s)),
        out_shape=jax.ShapeDtypeStruct((B, 2 * H * HEAD_ROWS, S), BF16),
        scratch_shapes=[
            pltpu.VMEM((tm, D), BF16),
            pltpu.VMEM((2, 3, H, tm), F32),
            pltpu.VMEM((H, 1), F32),
        ],
        compiler_params=_params(3),
        name="qk_proj",
    )(h, g.reshape(1, D), w_in_t, w_in_t, b_f.reshape(H, 1), tri,
      gq.reshape(HEAD_DIM, 1), gk.reshape(HEAD_DIM, 1), shift.reshape(1, 1))


def _vg_proj_kernel(x_ref, g_ref, w_ref, o_ref, xn_ref, *, n_v_steps):
    j = pl.program_id(2)

    @pl.when(j == 0)
    def _():
        xn_ref[...] = _rms(x_ref[...], g_ref[...]).astype(BF16)

    y = lax.dot_general(w_ref[...], xn_ref[...], (((1,), (1,)), ((), ())),
                        preferred_element_type=F32)

    @pl.when(j < n_v_steps)
    def _():
        o_ref[...] = y.astype(BF16)

    @pl.when(j >= n_v_steps)
    def _():
        o_ref[...] = jax.nn.sigmoid(y).astype(BF16)


def _vg_proj(h, g, w_in_t, layer):
    B, S, D = h.shape
    tm = _tile(S, PROJ_TOKENS)
    tn = _tile(D, PROJ_ROWS)
    n_v_steps = D // tn
    row0 = 2 * n_v_steps
    return pl.pallas_call(
        functools.partial(_vg_proj_kernel, n_v_steps=n_v_steps),
        grid=(B, S // tm, 2 * n_v_steps),
        in_specs=[
            pl.BlockSpec((None, tm, D), lambda b, s, j: (b, s, 0)),
            pl.BlockSpec((1, D), lambda b, s, j: (0, 0)),
            pl.BlockSpec((None, tn, D), lambda b, s, j: (layer, row0 + j, 0)),
        ],
        out_specs=pl.BlockSpec((None, tn, tm), lambda b, s, j: (b, j, s)),
        out_shape=jax.ShapeDtypeStruct((B, 2 * D, S), BF16),
        scratch_shapes=[pltpu.VMEM((tm, D), BF16)],
        compiler_params=_params(3),
        name="vg_proj",
    )(h, g.reshape(1, D), w_in_t)


def _attn_shifted_kernel(q_ref, k_ref, v_ref, gate_ref, o_ref, kn_ref, *, tq, tk, td):
    S = q_ref.shape[1]
    n_chunks = tq // td
    for c in range(S // td):
        cs = slice(c * td, (c + 1) * td)
        kn_ref[cs, :] = k_ref[:, cs].astype(F32).T.astype(BF16)

    kk = lax.broadcasted_iota(jnp.int32, (td, td), 0)
    qq = lax.broadcasted_iota(jnp.int32, (td, td), 1)
    causal = kk <= qq

    def probs(k_rows, q_cols, v_cols, mask):
        s_t = jnp.dot(kn_ref[k_rows, :], q_cols, preferred_element_type=F32)
        if mask:
            s_t = jnp.where(causal, s_t, MASKED_LOGIT)
        p = jnp.exp(s_t)
        pv = jnp.dot(v_ref[:, v_cols], p.astype(BF16), preferred_element_type=F32)
        return p, pv

    def row_sums8(p):
        return jnp.sum(p.reshape(p.shape[0] // 8, 8, p.shape[1]), axis=0)

    for qi in range(S // tq):
        q0 = qi * tq
        q_t = q_ref[:, q0:q0 + tq]
        acc = [jnp.zeros((HEAD_DIM, td), F32) for _ in range(n_chunks)]
        l8 = [jnp.zeros((8, td), F32) for _ in range(n_chunks)]
        for kj in range(q0 // tk):
            ks = slice(kj * tk, (kj + 1) * tk)
            p, pv = probs(ks, q_t, ks, mask=False)
            for c in range(n_chunks):
                cc = slice(c * td, (c + 1) * td)
                acc[c] = acc[c] + pv[:, cc]
                l8[c] = l8[c] + row_sums8(p[:, cc])
        for d in range(n_chunks):
            ks = slice(q0 + d * td, q0 + (d + 1) * td)
            p, pv = probs(ks, q_t[:, d * td:(d + 1) * td], ks, mask=True)
            acc[d] = acc[d] + pv
            l8[d] = l8[d] + row_sums8(p)
            if d + 1 < n_chunks:
                p, pv = probs(ks, q_t[:, (d + 1) * td:], ks, mask=False)
                for c in range(d + 1, n_chunks):
                    cc = slice((c - d - 1) * td, (c - d) * td)
                    acc[c] = acc[c] + pv[:, cc]
                    l8[c] = l8[c] + row_sums8(p[:, cc])
        for c in range(n_chunks):
            cs = slice(q0 + c * td, q0 + (c + 1) * td)
            l = jnp.sum(l8[c], axis=0, keepdims=True)
            o_ref[:, cs] = (acc[c] / l * gate_ref[:, cs].astype(F32)).astype(BF16)


def _attn_online_kernel(q_ref, k_ref, v_ref, gate_ref, o_ref, kn_ref, vb_ref, *, blk):
    S = q_ref.shape[1]
    n_blk = S // blk

    for c in range(n_blk):
        cs = slice(c * blk, (c + 1) * blk)
        kn_ref[c] = k_ref[:, cs].astype(F32).T.astype(BF16)
        vb_ref[c] = v_ref[:, cs]

    kk = lax.broadcasted_iota(jnp.int32, (blk, blk), 0)
    qq = lax.broadcasted_iota(jnp.int32, (blk, blk), 1)
    causal = kk <= qq

    def step(q_t, k_blk, v_t, carry, mask):
        m, l, acc = carry
        s_t = jnp.dot(k_blk, q_t, preferred_element_type=F32)
        if mask:
            s_t = jnp.where(causal, s_t, -jnp.inf)
        m_new = jnp.maximum(m, jnp.max(s_t, axis=0, keepdims=True))
        alpha = jnp.exp(m - m_new)
        p = jnp.exp(s_t - m_new)
        l = alpha * l + jnp.sum(p, axis=0, keepdims=True)
        acc = alpha * acc + jnp.dot(v_t, p.astype(BF16), preferred_element_type=F32)
        return m_new, l, acc

    for qi in range(n_blk):
        qs = slice(qi * blk, (qi + 1) * blk)
        q_t = q_ref[:, qs]
        carry = (jnp.full((1, blk), -jnp.inf, F32), jnp.zeros((1, blk), F32),
                 jnp.zeros((HEAD_DIM, blk), F32))

        def body(kj, carry, q_t=q_t):
            return step(q_t, kn_ref[kj], vb_ref[kj], carry, mask=False)

        carry = lax.fori_loop(0, qi, body, carry)
        m, l, acc = step(q_t, kn_ref[qi], vb_ref[qi], carry, mask=True)
        o_ref[:, qs] = (acc / l * gate_ref[:, qs].astype(F32)).astype(BF16)


def _attention(qk_t, vg_t, n_heads, online_max):
    B, _, S = qk_t.shape
    D = n_heads * HEAD_DIM
    if online_max:
        blk = _tile(S, ATTN_ONLINE_BLOCK)
        body = functools.partial(_attn_online_kernel, blk=blk)
        scratch = [pltpu.VMEM((S // blk, blk, HEAD_ROWS), BF16),
                   pltpu.VMEM((S // blk, HEAD_DIM, blk), BF16)]
    else:
        tq = _tile(S, ATTN_QUERIES)
        tk = _tile(tq, ATTN_KEYS)
        td = _tile(tk, ATTN_DIAG)
        body = functools.partial(_attn_shifted_kernel, tq=tq, tk=tk, td=td)
        scratch = [pltpu.VMEM((S, HEAD_ROWS), BF16)]
    return pl.pallas_call(
        body,
        grid=(B, n_heads),
        in_specs=[
            pl.BlockSpec((None, HEAD_ROWS, S), lambda b, h: (b, h, 0)),
            pl.BlockSpec((None, HEAD_ROWS, S), lambda b, h: (b, h + n_heads, 0)),
            pl.BlockSpec((None, HEAD_DIM, S), lambda b, h: (b, h, 0)),
            pl.BlockSpec((None, HEAD_DIM, S), lambda b, h: (b, h + n_heads, 0)),
        ],
        out_specs=pl.BlockSpec((None, HEAD_DIM, S), lambda b, h: (b, h, 0)),
        out_shape=jax.ShapeDtypeStruct((B, D, S), BF16),
        scratch_shapes=scratch,
        compiler_params=_params(2),
        name="fox_attn_online" if online_max else "fox_attn",
    )(qk_t, qk_t, vg_t, vg_t)


def _out_proj_kernel(o_ref, w_ref, x_ref, y_ref):
    y_t = jnp.dot(w_ref[...], o_ref[...], preferred_element_type=F32)
    y_ref[...] = x_ref[...] + y_t.T


def _out_proj(o_t, w_out_t, layer, h):
    B, S, D = h.shape
    tm = _tile(S, OUT_TOKENS)
    return pl.pallas_call(
        _out_proj_kernel,
        grid=(B, S // tm),
        in_specs=[
            pl.BlockSpec((None, D, tm), lambda b, s: (b, 0, s)),
            pl.BlockSpec((None, D, D), lambda b, s: (layer, 0, 0)),
            pl.BlockSpec((None, tm, D), lambda b, s: (b, s, 0)),
        ],
        out_specs=pl.BlockSpec((None, tm, D), lambda b, s: (b, s, 0)),
        out_shape=jax.ShapeDtypeStruct((B, S, D), F32),
        compiler_params=_params(2),
        name="out_proj",
    )(o_t, w_out_t, h)


def _transpose_cast_kernel(w_ref, o_ref):
    o_ref[...] = w_ref[...].T.astype(BF16)


def _transpose_cast(w, n_cols):
    L, K, _ = w.shape
    tn = _tile(n_cols, WEIGHT_COLS)
    return pl.pallas_call(
        _transpose_cast_kernel,
        grid=(L, n_cols // tn),
        in_specs=[pl.BlockSpec((None, K, tn), lambda l, j: (l, 0, j))],
        out_specs=pl.BlockSpec((None, tn, K), lambda l, j: (l, j, 0)),
        out_shape=jax.ShapeDtypeStruct((L, n_cols, K), BF16),
        compiler_params=_params(2),
        name="weight_transpose",
    )(w)


def _fox_layer(h, g, w_in_t, b_f, gq, gk, w_out_t, layer):
    B, S, D = h.shape
    H = D // HEAD_DIM
    bound = (HEAD_DIM * ATTN_SCALE * BOUND_SLACK) * jnp.max(jnp.abs(gq)) * jnp.max(jnp.abs(gk))
    shifted = bound <= MAX_SHIFT
    shift = jnp.where(shifted, bound, 0.0).astype(F32)
    qk_t = _qk_proj(h, g, w_in_t, layer, b_f, gq, gk, shift)
    vg_t = _vg_proj(h, g, w_in_t, layer)
    o_t = lax.cond(shifted,
                   functools.partial(_attention, n_heads=H, online_max=False),
                   functools.partial(_attention, n_heads=H, online_max=True),
                   qk_t, vg_t)
    return _out_proj(o_t, w_out_t, layer, h)


def kernel(x, attn_norm_g, ffn_norm_g, final_norm_g, pool_w, pool_scale, fox_w_in, fox_b_f,
           fox_q_norm_g, fox_k_norm_g, fox_w_out, ffn_w_gate_up, ffn_w_down):
    B, S, D = x.shape
    depth = attn_norm_g.shape[0]
    fox_w_in_t = jnp.swapaxes(fox_w_in, 1, 2).astype(BF16)
    fox_w_out_t = _transpose_cast(fox_w_out, D)
    w_gu = ffn_w_gate_up[:1].astype(BF16)
    w_d = ffn_w_down[:1].astype(BF16)
    pool_w16 = pool_w.astype(BF16)
    h = x
    for i in range(depth):
        j = i // 2
        if i % 2 == 0:
            h = _pool_layer(h, attn_norm_g[i], pool_w16, j, pool_scale[j])
        else:
            h = _fox_layer(h, attn_norm_g[i], fox_w_in_t, fox_b_f[j], fox_q_norm_g[j],
                           fox_k_norm_g[j], fox_w_out_t, j)
        if i == depth - 1:
            h = _ffn(h.reshape(B * S, D), ffn_norm_g[i], w_gu, w_d, 0, final_g=final_norm_g)
        else:
            h, w_gu, w_d = _ffn(h.reshape(B * S, D), ffn_norm_g[i], w_gu, w_d, 0,
                                cast_next=(ffn_w_gate_up, ffn_w_down, i + 1))
        h = h.reshape(B, S, D)
    return h
```
